```python
import math
import jax, jax.numpy as jnp
from jax import lax
import numpy as np

D_MODEL = 1024
BATCH = 4
SEQ = 4096
DEPTH = 1
DEC_BATCH = 128
DEC_SEQ = 4
PAST_LEN = 8192
PAGE_SIZE = 128

N_HEADS = 8
HEAD_DIM = 64
V_DIM = 2 * HEAD_DIM
ATTN_W = N_HEADS * 2 * HEAD_DIM
ROPE_THETA = 10000.0
Q_BLOCK = 128
POOL_WINDOWS = (2, 4, 8, 16)
POOL_GROUPS = len(POOL_WINDOWS)
POOL_GC = 128
POOL_W = POOL_GROUPS * POOL_GC
POOL_BUF = max(POOL_WINDOWS) - 1
N_MEM = 256
MEM_HEADS = 4
MEM_HEAD_DIM = 128
MEM_W = MEM_HEADS * MEM_HEAD_DIM
N_BRANCH = 3
D_FF = -(-8 * D_MODEL // (3 * 256)) * 256
IN_W = 3 * ATTN_W + POOL_W + MEM_W + N_BRANCH * D_MODEL
IN_SPLITS = (ATTN_W, 2 * ATTN_W, 3 * ATTN_W, 3 * ATTN_W + POOL_W, 3 * ATTN_W + POOL_W + MEM_W)
EPS = 1e-5
NEG_INF = -1e30

kernel_name = "hybrid_diffattn_pool_memxattn_decode_step"


def rms_norm(x, g):
    x32 = x.astype(jnp.float32)
    y = x32 * lax.rsqrt(jnp.mean(x32 * x32, axis=-1, keepdims=True) + EPS)
    return (y * g.astype(jnp.float32)).astype(x.dtype)


def rotary(x, pos):
    half = HEAD_DIM // 2
    inv = ROPE_THETA ** (-jnp.arange(half, dtype=jnp.float32) / half)
    ang = pos.astype(jnp.float32)[:, None] * inv[None, :]
    cos = jnp.cos(ang)[None, :, None, None, :]
    sin = jnp.sin(ang)[None, :, None, None, :]
    x32 = x.astype(jnp.float32)
    x1, x2 = x32[..., :half], x32[..., half:]
    return jnp.concatenate([x1 * cos - x2 * sin, x2 * cos + x1 * sin], axis=-1).astype(x.dtype)


def diff_lambda(lq1, lk1, lq2, lk2, lam_init):
    f = jnp.float32
    return (jnp.exp(jnp.sum(lq1.astype(f) * lk1.astype(f)))
            - jnp.exp(jnp.sum(lq2.astype(f) * lk2.astype(f))) + lam_init)


def diff_core(q, qpos, key_segs, val_segs, kpos_segs, lam):
    scores = []
    for k, kp in zip(key_segs, kpos_segs):
        s = jnp.einsum('bqhcd,bkhcd->bhcqk', q, k).astype(jnp.float32)
        mask = kp[None, :] <= qpos[:, None]
        scores.append(jnp.where(mask[None, None, None], s, NEG_INF))
    p = jax.nn.softmax(jnp.concatenate(scores, axis=-1), axis=-1)
    a = p[:, :, 0] - lam * p[:, :, 1]
    out = None
    off = 0
    for v in val_segs:
        n = v.shape[1]
        o = jnp.einsum('bhqk,bkhe->bqhe', a[..., off:off + n].astype(v.dtype), v)
        out = o if out is None else out + o
        off += n
    return out


def diff_attention(q, qpos, key_segs, val_segs, kpos_segs, lam):
    B, Sq = q.shape[:2]
    qb = min(Q_BLOCK, Sq)
    nb = Sq // qb
    qs = q.reshape(B, nb, qb, N_HEADS, 2, HEAD_DIM).swapaxes(0, 1)
    ps = qpos.reshape(nb, qb)
    out = lax.map(lambda a: diff_core(a[0], a[1], key_segs, val_segs, kpos_segs, lam), (qs, ps))
    return out.swapaxes(0, 1).reshape(B, Sq, N_HEADS, V_DIM)


def pool_mix(u, prefix, n_prior, pool_w, pool_scale):
    B, S, _ = u.shape
    ext = jnp.concatenate([prefix.astype(u.dtype), u], axis=1)
    ext32 = ext.astype(jnp.float32)
    cs = jnp.concatenate([jnp.zeros((B, 1, POOL_W), jnp.float32), jnp.cumsum(ext32, axis=1)], axis=1)
    t = jnp.arange(S, dtype=jnp.float32)
    outs = []
    for gi, w in enumerate(POOL_WINDOWS):
        sl = slice(gi * POOL_GC, (gi + 1) * POOL_GC)
        hi = cs[:, POOL_BUF + 1:POOL_BUF + 1 + S, sl]
        lo = cs[:, POOL_BUF + 1 - w:POOL_BUF + 1 - w + S, sl]
        cnt = jnp.minimum(float(w), n_prior + 1.0 + t)[None, :, None]
        d = (hi - lo) / cnt - ext32[:, POOL_BUF:, sl]
        outs.append(jnp.einsum('bsc,cd->bsd', d.astype(u.dtype), pool_w[gi]))
    y = jnp.concatenate(outs, axis=-1) * pool_scale
    return y, ext[:, -POOL_BUF:]


def mem_kv(mem, norm_mem_g, w_mem_kv):
    B = mem.shape[0]
    kv = rms_norm(mem, norm_mem_g) @ w_mem_kv
    k, v = jnp.split(kv, 2, axis=-1)
    return (k.reshape(B, N_MEM, MEM_HEADS, MEM_HEAD_DIM), v.reshape(B, N_MEM, MEM_HEADS, MEM_HEAD_DIM))


def mem_attention(qm, mk, mv):
    s = jnp.einsum('bshd,bnhd->bhsn', qm, mk).astype(jnp.float32) * (MEM_HEAD_DIM ** -0.5)
    p = jax.nn.softmax(s, axis=-1)
    return jnp.einsum('bhsn,bnhd->bshd', p.astype(mv.dtype), mv)


def trunk_layer(x, pos, past_k, past_v, pool_prefix, pool_prior, mem_k, mem_v, lam_init,
                norm1_g, w_in, lambda_q1, lambda_k1, lambda_q2, lambda_k2, subln_g,
                pool_w, pool_scale, w_attn_proj, w_pool_proj, w_mem_proj, w_out,
                norm2_g, w_ffn_gate, w_ffn_up, w_ffn_down):
    B, S, _ = x.shape
    h = rms_norm(x, norm1_g)
    z = h @ w_in
    q, k, v, u, qm, g = jnp.split(z, IN_SPLITS, axis=-1)
    q = rotary(q.reshape(B, S, N_HEADS, 2, HEAD_DIM), pos) * (HEAD_DIM ** -0.5)
    k = rotary(k.reshape(B, S, N_HEADS, 2, HEAD_DIM), pos)
    v = v.reshape(B, S, N_HEADS, V_DIM)
    lam = diff_lambda(lambda_q1, lambda_k1, lambda_q2, lambda_k2, lam_init)
    if past_k is None:
        keys, vals, kpos = [k], [v], [pos]
    else:
        P = past_k.shape[1]
        keys = [past_k.reshape(B, P, N_HEADS, 2, HEAD_DIM), k]
        vals = [past_v, v]
        kpos = [jnp.arange(P, dtype=jnp.int32), pos]
    o = diff_attention(q, pos, keys, vals, kpos, lam)
    o = rms_norm(o, subln_g) * (1.0 - lam_init)
    o_attn = o.reshape(B, S, ATTN_W) @ w_attn_proj
    o_pool, pool_state = pool_mix(u, pool_prefix, pool_prior, pool_w, pool_scale)
    o_pool = o_pool @ w_pool_proj
    o_mem = mem_attention(qm.reshape(B, S, MEM_HEADS, MEM_HEAD_DIM), mem_k, mem_v).reshape(B, S, MEM_W) @ w_mem_proj
    ga, gp, gm = jnp.split(jax.nn.sigmoid(g.astype(jnp.float32)).astype(x.dtype), N_BRANCH, axis=-1)
    x = x + (ga * o_attn + gp * o_pool + gm * o_mem) @ w_out
    h2 = rms_norm(x, norm2_g)
    x = x + (jax.nn.silu(h2 @ w_ffn_gate) * (h2 @ w_ffn_up)) @ w_ffn_down
    return x, k.reshape(B, S, N_HEADS, 2 * HEAD_DIM), v, pool_state


def setup_inputs(seed: int = 0) -> dict:
    key = jax.random.key(seed)
    ks = iter(jax.random.split(key, 40))
    f = jnp.float32
    n_pages = PAST_LEN // PAGE_SIZE
    n_used = DEC_BATCH * n_pages
    n_pool = n_used + max(1, n_used // 4)

    def nrm(shape, scale=1.0):
        return jax.random.normal(next(ks), shape, f) * scale

    def gain(shape):
        return 1.0 + 0.02 * jax.random.normal(next(ks), shape, f)

    page_table = jax.random.permutation(next(ks), n_pool)[:n_used].reshape(DEC_BATCH, n_pages).astype(jnp.int32)
    return {
        "x_prompt": nrm((BATCH, SEQ, D_MODEL)),
        "x_sample": nrm((DEC_BATCH, DEC_SEQ, D_MODEL)),
        "cache_k": nrm((DEPTH, n_pool, PAGE_SIZE, N_HEADS, 2 * HEAD_DIM)),
        "cache_v": nrm((DEPTH, n_pool, PAGE_SIZE, N_HEADS, V_DIM)),
        "page_table": page_table,
        "state_pool": nrm((DEPTH, DEC_BATCH, POOL_BUF, POOL_W)),
        "cache_mem_k": nrm((DEPTH, DEC_BATCH, N_MEM, MEM_HEADS, MEM_HEAD_DIM)),
        "cache_mem_v": nrm((DEPTH, DEC_BATCH, N_MEM, MEM_HEADS, MEM_HEAD_DIM)),
        "mem_prompt": nrm((BATCH, N_MEM, D_MODEL)),
        "norm1_g": gain((DEPTH, D_MODEL)),
        "w_in": nrm((DEPTH, D_MODEL, IN_W), D_MODEL ** -0.5),
        "lambda_q1": nrm((DEPTH, HEAD_DIM), 0.1),
        "lambda_k1": nrm((DEPTH, HEAD_DIM), 0.1),
        "lambda_q2": nrm((DEPTH, HEAD_DIM), 0.1),
        "lambda_k2": nrm((DEPTH, HEAD_DIM), 0.1),
        "subln_g": gain((DEPTH, V_DIM)),
        "pool_w": nrm((DEPTH, POOL_GROUPS, POOL_GC, POOL_GC), POOL_GC ** -0.5),
        "pool_scale": gain((DEPTH, POOL_W)),
        "norm_mem_g": gain((DEPTH, D_MODEL)),
        "w_mem_kv": nrm((DEPTH, D_MODEL, 2 * MEM_W), D_MODEL ** -0.5),
        "w_attn_proj": nrm((DEPTH, ATTN_W, D_MODEL), ATTN_W ** -0.5),
        "w_pool_proj": nrm((DEPTH, POOL_W, D_MODEL), POOL_W ** -0.5),
        "w_mem_proj": nrm((DEPTH, MEM_W, D_MODEL), MEM_W ** -0.5),
        "w_out": nrm((DEPTH, D_MODEL, D_MODEL), D_MODEL ** -0.5),
        "norm2_g": gain((DEPTH, D_MODEL)),
        "w_ffn_gate": nrm((DEPTH, D_MODEL, D_FF), D_MODEL ** -0.5),
        "w_ffn_up": nrm((DEPTH, D_MODEL, D_FF), D_MODEL ** -0.5),
        "w_ffn_down": nrm((DEPTH, D_FF, D_MODEL), D_FF ** -0.5),
        "norm_f_g": gain((D_MODEL,)),
    }


def reference(x_prompt, x_sample, cache_k, cache_v, page_table, state_pool, cache_mem_k, cache_mem_v,
              mem_prompt, norm1_g, w_in, lambda_q1, lambda_k1, lambda_q2, lambda_k2, subln_g,
              pool_w, pool_scale, norm_mem_g, w_mem_kv, w_attn_proj, w_pool_proj, w_mem_proj, w_out,
              norm2_g, w_ffn_gate, w_ffn_up, w_ffn_down, norm_f_g):
    Bp, Sp, _ = x_prompt.shape
    Bs, Ss, _ = x_sample.shape
    past_len = page_table.shape[1] * cache_k.shape[2]
    pos_p = jnp.arange(Sp, dtype=jnp.int32)
    pos_s = past_len + jnp.arange(Ss, dtype=jnp.int32)
    xp, xs = x_prompt, x_sample
    kp_l, vp_l, pp_l, mkp_l, mvp_l, ks_l, vs_l, ps_l = [], [], [], [], [], [], [], []
    for l in range(DEPTH):
        lam_init = 0.8 - 0.6 * math.exp(-0.3 * l)
        w = (norm1_g[l], w_in[l], lambda_q1[l], lambda_k1[l], lambda_q2[l], lambda_k2[l], subln_g[l],
             pool_w[l], pool_scale[l], w_attn_proj[l], w_pool_proj[l], w_mem_proj[l], w_out[l],
             norm2_g[l], w_ffn_gate[l], w_ffn_up[l], w_ffn_down[l])
        mk_p, mv_p = mem_kv(mem_prompt, norm_mem_g[l], w_mem_kv[l])
        zeros_pool = jnp.zeros((Bp, POOL_BUF, POOL_W), xp.dtype)
        xp, k_p, v_p, pool_p = trunk_layer(xp, pos_p, None, None, zeros_pool, 0, mk_p, mv_p, lam_init, *w)
        past_k = cache_k[l, page_table].reshape(Bs, past_len, N_HEADS, 2 * HEAD_DIM)
        past_v = cache_v[l, page_table].reshape(Bs, past_len, N_HEADS, V_DIM)
        xs, k_s, v_s, pool_s = trunk_layer(xs, pos_s, past_k, past_v, state_pool[l], POOL_BUF,
                                           cache_mem_k[l], cache_mem_v[l], lam_init, *w)
        kp_l.append(k_p); vp_l.append(v_p); pp_l.append(pool_p); mkp_l.append(mk_p); mvp_l.append(mv_p)
        ks_l.append(k_s); vs_l.append(v_s); ps_l.append(pool_s)
    y_prompt = rms_norm(xp, norm_f_g)
    y_sample = rms_norm(xs, norm_f_g)
    return (y_prompt, y_sample,
            jnp.stack(kp_l), jnp.stack(vp_l), jnp.stack(pp_l), jnp.stack(mkp_l), jnp.stack(mvp_l),
            jnp.stack(ks_l), jnp.stack(vs_l), jnp.stack(ps_l))
```

```python
import functools
import math

import jax
import jax.numpy as jnp
from jax import lax
from jax.experimental import pallas as pl
from jax.experimental.pallas import tpu as pltpu

F32 = jnp.float32
BF16 = jnp.bfloat16

N_HEADS = 8
HEAD_DIM = 64
V_DIM = 2 * HEAD_DIM
ATTN_W = N_HEADS * V_DIM
ROPE_THETA = 10000.0
POOL_WINDOWS = (2, 4, 8, 16)
POOL_GC = 128
POOL_W = len(POOL_WINDOWS) * POOL_GC
POOL_BUF = max(POOL_WINDOWS) - 1
POOL_HALO = 16
MEM_HEADS = 4
MEM_HEAD_DIM = 128
MEM_W = MEM_HEADS * MEM_HEAD_DIM
EPS = 1e-5
NEG_INF = -1e30
Q_SCALE = HEAD_DIM ** -0.5
MEM_SCALE = MEM_HEAD_DIM ** -0.5

VMEM_LIMIT = 56 * 1024 * 1024


def _cparams(sem):
    return pltpu.CompilerParams(dimension_semantics=sem, vmem_limit_bytes=VMEM_LIMIT)


def _rms(x, g):
    return x * lax.rsqrt(jnp.mean(x * x, axis=-1, keepdims=True) + EPS) * g


def _dot(a, b):
    return jnp.dot(a, b, preferred_element_type=F32)


def _dot_nt(a, b):
    return lax.dot_general(a, b, (((1,), (1,)), ((), ())), preferred_element_type=F32)


def _row_tile(n, pref):
    t = min(pref, n)
    while n % t:
        t //= 2
    return t


def _in_proj_kernel(x_ref, g_ref, w_ref, cos_ref, sin_ref,
                    q_ref, kf_ref, kb_ref, vf_ref, vb_ref, u_ref, qm_ref):
    h = _rms(x_ref[...], g_ref[...]).astype(BF16)
    cos = cos_ref[...]
    sin = sin_ref[...]
    lane = lax.broadcasted_iota(jnp.int32, cos.shape, 1)
    first_half = (lane & (HEAD_DIM - 1)) < HEAD_DIM // 2

    def rotary(zh):
        partner = jnp.where(first_half,
                            pltpu.roll(zh, V_DIM - HEAD_DIM // 2, axis=1),
                            pltpu.roll(zh, HEAD_DIM // 2, axis=1))
        return zh * cos + partner * sin

    zq = _dot(h, w_ref[:, 0:ATTN_W])
    for hd in range(N_HEADS):
        sl = slice(hd * V_DIM, (hd + 1) * V_DIM)
        q_ref[:, sl] = (rotary(zq[:, sl]) * Q_SCALE).astype(BF16)
    zk = _dot(h, w_ref[:, ATTN_W:2 * ATTN_W])
    for hd in range(N_HEADS):
        sl = slice(hd * V_DIM, (hd + 1) * V_DIM)
        kr = rotary(zk[:, sl])
        kf_ref[:, sl] = kr
        kb_ref[:, sl] = kr.astype(BF16)
    zv = _dot(h, w_ref[:, 2 * ATTN_W:3 * ATTN_W])
    vf_ref[...] = zv
    vb_ref[...] = zv.astype(BF16)
    zu = _dot(h, w_ref[:, 3 * ATTN_W:3 * ATTN_W + POOL_W + MEM_W])
    u_ref[...] = zu[:, :POOL_W]
    qm_ref[...] = zu[:, POOL_W:].astype(BF16)


def _in_proj(x, g1, w, cos_t, sin_t):
    T, D = x.shape
    P = cos_t.shape[0]
    tm = _row_tile(math.gcd(T, P), 256)
    npos = P // tm
    wcols = w.shape[1]
    row = lambda i: (i, 0)
    const = lambda i: (0, 0)
    posmap = lambda i: (i % npos, 0)
    outs = (
        jax.ShapeDtypeStruct((T, ATTN_W), BF16),
        jax.ShapeDtypeStruct((T, ATTN_W), F32),
        jax.ShapeDtypeStruct((T, ATTN_W), BF16),
        jax.ShapeDtypeStruct((T, ATTN_W), F32),
        jax.ShapeDtypeStruct((T, ATTN_W), BF16),
        jax.ShapeDtypeStruct((T, POOL_W), F32),
        jax.ShapeDtypeStruct((T, MEM_W), BF16),
    )
    return pl.pallas_call(
        _in_proj_kernel,
        out_shape=outs,
        grid=(T // tm,),
        in_specs=[
            pl.BlockSpec((tm, D), row),
            pl.BlockSpec((1, D), const),
            pl.BlockSpec((D, wcols), const),
            pl.BlockSpec((tm, V_DIM), posmap),
            pl.BlockSpec((tm, V_DIM), posmap),
        ],
        out_specs=(
            pl.BlockSpec((tm, ATTN_W), row),
            pl.BlockSpec((tm, ATTN_W), row),
            pl.BlockSpec((tm, ATTN_W), row),
            pl.BlockSpec((tm, ATTN_W), row),
            pl.BlockSpec((tm, ATTN_W), row),
            pl.BlockSpec((tm, POOL_W), row),
            pl.BlockSpec((tm, MEM_W), row),
        ),
        compiler_params=_cparams(("parallel",)),
        name="in_proj",
    )(x, g1, w, cos_t, sin_t)


def _diff_lambda(lq1_ref, lk1_ref, lq2_ref, lk2_ref, lam_init):
    a = jnp.sum(lq1_ref[...] * lk1_ref[...], axis=-1, keepdims=True)
    b = jnp.sum(lq2_ref[...] * lk2_ref[...], axis=-1, keepdims=True)
    return jnp.exp(a) - jnp.exp(b) + lam_init


def _sub_norm(o, g, lam_init):
    return _rms(o, g) * (1.0 - lam_init)


def _prompt_attn_kernel(q_ref, k_ref, v_ref, lq1_ref, lk1_ref, lq2_ref, lk2_ref, sg_ref, o_ref,
                        *, tq, lam_init):
    qi = pl.program_id(2)
    q = q_ref[...]
    lane = lax.broadcasted_iota(jnp.int32, q.shape, 1)
    zero = jnp.zeros_like(q)
    qq = jnp.concatenate([jnp.where(lane < HEAD_DIM, q, zero),
                          jnp.where(lane >= HEAD_DIM, q, zero)], axis=0)

    def step(j, carry, masked):
        m, l, acc = carry
        kb = k_ref[pl.ds(pl.multiple_of(j * tq, tq), tq), :]
        vb = v_ref[pl.ds(pl.multiple_of(j * tq, tq), tq), :]
        s = _dot_nt(qq, kb)
        if masked:
            r = lax.broadcasted_iota(jnp.int32, s.shape, 0)
            c = lax.broadcasted_iota(jnp.int32, s.shape, 1)
            r = jnp.where(r >= tq, r - tq, r)
            s = jnp.where(c <= r, s, NEG_INF)
        m_new = jnp.maximum(m, jnp.max(s, axis=-1, keepdims=True))
        alpha = jnp.exp(m - m_new)
        p = jnp.exp(s - m_new)
        l = alpha * l + jnp.sum(p, axis=-1, keepdims=True)
        acc = alpha * acc + _dot(p.astype(BF16), vb)
        return m_new, l, acc

    init = (jnp.full((2 * tq, 1), NEG_INF, F32),
            jnp.zeros((2 * tq, 1), F32),
            jnp.zeros((2 * tq, V_DIM), F32))
    carry = lax.fori_loop(0, qi, lambda j, c: step(j, c, False), init)
    m, l, acc = step(qi, carry, True)
    on = acc / l
    lam = _diff_lambda(lq1_ref, lk1_ref, lq2_ref, lk2_ref, lam_init)
    o = on[:tq] - lam * on[tq:]
    o_ref[...] = _sub_norm(o, sg_ref[...], lam_init).astype(o_ref.dtype)


def _prompt_attn(q, k, v, lq1, lk1, lq2, lk2, sg, B, S, lam_init):
    T = q.shape[0]
    tq = _row_tile(S, 512)
    nq = S // tq
    vec = lambda b, h, i: (0, 0)
    kern = functools.partial(_prompt_attn_kernel, tq=tq, lam_init=lam_init)
    return pl.pallas_call(
        kern,
        out_shape=jax.ShapeDtypeStruct((T, ATTN_W), BF16),
        grid=(B, N_HEADS, nq),
        in_specs=[
            pl.BlockSpec((tq, V_DIM), lambda b, h, i: (b * nq + i, h)),
            pl.BlockSpec((S, V_DIM), lambda b, h, i: (b, h)),
            pl.BlockSpec((S, V_DIM), lambda b, h, i: (b, h)),
            pl.BlockSpec((1, HEAD_DIM), vec),
            pl.BlockSpec((1, HEAD_DIM), vec),
            pl.BlockSpec((1, HEAD_DIM), vec),
            pl.BlockSpec((1, HEAD_DIM), vec),
            pl.BlockSpec((1, V_DIM), vec),
        ],
        out_specs=pl.BlockSpec((tq, V_DIM), lambda b, h, i: (b * nq + i, h)),
        compiler_params=_cparams(("parallel", "parallel", "arbitrary")),
        name="prompt_attn",
    )(q, k, v, lq1, lk1, lq2, lk2, sg)


def _sample_attn_kernel(pt_ref, q_ref, kn_ref, vn_ref, lq1_ref, lk1_ref, lq2_ref, lk2_ref, sg_ref,
                        *rest, n_grp, lam_init):
    del pt_ref
    k_refs = rest[:n_grp]
    v_refs = rest[n_grp:2 * n_grp]
    o_ref = rest[2 * n_grp]
    qrows_ref, m_ref, l_ref, acc_ref = rest[2 * n_grp + 1:]
    g = pl.program_id(1)

    @pl.when(g == 0)
    def _():
        q8 = q_ref[0]
        r8 = lax.broadcasted_iota(jnp.int32, (8, V_DIM), 0)
        c8 = lax.broadcasted_iota(jnp.int32, (8, V_DIM), 1)
        own_map = (c8 >= HEAD_DIM) == (r8 >= 4)
        qrows_ref[...] = jnp.concatenate(
            [jnp.where(own_map, q8[:, hd * V_DIM:(hd + 1) * V_DIM], 0.0) for hd in range(N_HEADS)],
            axis=0).astype(BF16)
        m_ref[...] = jnp.full(m_ref.shape, NEG_INF, F32)
        l_ref[...] = jnp.zeros(l_ref.shape, F32)
        acc_ref[...] = jnp.zeros(acc_ref.shape, F32)

    qrows = qrows_ref[...]

    def own_head(shape):
        rj = lax.broadcasted_iota(jnp.int32, shape, 0)
        cj = lax.broadcasted_iota(jnp.int32, shape, 1)
        return (cj & (N_HEADS - 1)) == (rj >> 3), rj, cj

    def update(s, vals):
        m = m_ref[...]
        m_new = jnp.maximum(m, jnp.max(s, axis=-1, keepdims=True))
        alpha = jnp.exp(m - m_new)
        p = jnp.exp(s - m_new)
        l_ref[...] = alpha * l_ref[...] + jnp.sum(p, axis=-1, keepdims=True)
        pb = p.astype(BF16)
        n = s.shape[1] // len(vals)
        pv = _dot(pb[:, 0:n], vals[0])
        for i in range(1, len(vals)):
            pv = pv + _dot(pb[:, i * n:(i + 1) * n], vals[i])
        acc_ref[...] = alpha * acc_ref[...] + pv
        m_ref[...] = m_new

    s = jnp.concatenate([_dot_nt(qrows, k_refs[i][0].astype(BF16)) for i in range(n_grp)], axis=1)
    own, _, _ = own_head(s.shape)
    update(jnp.where(own, s, NEG_INF), [v_refs[i][0].astype(BF16) for i in range(n_grp)])

    @pl.when(g == pl.num_programs(1) - 1)
    def _():
        sn = _dot_nt(qrows, kn_ref[0])
        own, rj, cj = own_head(sn.shape)
        update(jnp.where(own & ((cj >> 3) <= (rj & 3)), sn, NEG_INF), [vn_ref[0]])
        lam = _diff_lambda(lq1_ref, lk1_ref, lq2_ref, lk2_ref, lam_init)
        on = acc_ref[...] / l_ref[...]
        for hd in range(N_HEADS):
            blk = on[hd * 8:(hd + 1) * 8]
            o8 = blk - lam * pltpu.roll(blk, 4, axis=0)
            o_ref[0, :, hd * V_DIM:(hd + 1) * V_DIM] = _sub_norm(o8, sg_ref[...], lam_init)


def _sample_attn(page_table, q8, kn, vn, lq1, lk1, lq2, lk2, sg, ck, cv, lam_init):
    Bs, n_pages = page_table.shape
    page_rows = ck.shape[1]
    new_rows = kn.shape[1]
    n_grp = 8
    while n_pages % n_grp:
        n_grp //= 2
    vec = lambda b, g, pt: (0, 0)
    seq = lambda b, g, pt: (b, 0, 0)

    def page_spec(i):
        return pl.BlockSpec((1, page_rows, V_DIM), lambda b, g, pt: (pt[b, g * n_grp + i], 0, 0))

    kern = functools.partial(_sample_attn_kernel, n_grp=n_grp, lam_init=lam_init)
    grid_spec = pltpu.PrefetchScalarGridSpec(
        num_scalar_prefetch=1,
        grid=(Bs, n_pages // n_grp),
        in_specs=[
            pl.BlockSpec((1, 8, ATTN_W), seq),
            pl.BlockSpec((1, new_rows, V_DIM), seq),
            pl.BlockSpec((1, new_rows, V_DIM), seq),
            pl.BlockSpec((1, HEAD_DIM), vec),
            pl.BlockSpec((1, HEAD_DIM), vec),
            pl.BlockSpec((1, HEAD_DIM), vec),
            pl.BlockSpec((1, HEAD_DIM), vec),
            pl.BlockSpec((1, V_DIM), vec),
        ] + [page_spec(i) for i in range(n_grp)] + [page_spec(i) for i in range(n_grp)],
        out_specs=pl.BlockSpec((1, 8, ATTN_W), seq),
        scratch_shapes=[
            pltpu.VMEM((N_HEADS * 8, V_DIM), BF16),
            pltpu.VMEM((N_HEADS * 8, 1), F32),
            pltpu.VMEM((N_HEADS * 8, 1), F32),
            pltpu.VMEM((N_HEADS * 8, V_DIM), F32),
        ],
    )
    return pl.pallas_call(
        kern,
        out_shape=jax.ShapeDtypeStruct((Bs, 8, ATTN_W), F32),
        grid_spec=grid_spec,
        compiler_params=_cparams(("parallel", "arbitrary")),
        name="sample_attn",
    )(page_table, q8, kn, vn, lq1, lk1, lq2, lk2, sg, *([ck] * n_grp), *([cv] * n_grp))


def _pool_rows(ext, cnt_fn, pw_ref, ps_ref):
    outs = []
    for gi, w in enumerate(POOL_WINDOWS):
        tok = ext[:, gi * POOL_GC:(gi + 1) * POOL_GC]
        cur = tok
        sh = 1
        while sh < w:
            cur = cur + pltpu.roll(cur, sh, axis=0)
            sh *= 2
        d = cur / cnt_fn(w) - tok
        outs.append(_dot(d.astype(BF16), pw_ref[gi]))
    return jnp.concatenate(outs, axis=-1) * ps_ref[...]


def _mem_attend(qm, mk, mv):
    outs = []
    for hd in range(MEM_HEADS):
        sl = slice(hd * MEM_HEAD_DIM, (hd + 1) * MEM_HEAD_DIM)
        s = _dot_nt(qm[:, sl], mk[:, sl]) * MEM_SCALE
        s = s - jnp.max(s, axis=-1, keepdims=True)
        e = jnp.exp(s)
        p = e / jnp.sum(e, axis=-1, keepdims=True)
        outs.append(_dot(p.astype(BF16), mv[:, sl]))
    return jnp.concatenate(outs, axis=-1)


def _prompt_branch_kernel(u_ref, halo_ref, qm_ref, mk_ref, mv_ref, pw_ref, ps_ref, op_ref, om_ref, *, tm):
    i = pl.program_id(1)
    halo = jnp.where(i > 0, halo_ref[...], 0.0)
    ext = jnp.concatenate([halo, u_ref[...]], axis=0)
    pos = i * tm - POOL_HALO + lax.broadcasted_iota(jnp.int32, (POOL_HALO + tm, 1), 0)
    cnt_fn = lambda w: jnp.clip(pos + 1, 1, w).astype(F32)
    y = _pool_rows(ext, cnt_fn, pw_ref, ps_ref)
    op_ref[...] = y[POOL_HALO:].astype(op_ref.dtype)
    om_ref[...] = _mem_attend(qm_ref[...], mk_ref[...].astype(BF16),
                              mv_ref[...].astype(BF16)).astype(om_ref.dtype)


def _prompt_branch(u, qm, mk, mv, pw, ps, B, S):
    T = u.shape[0]
    n_mem = mk.shape[0] // B
    tm = _row_tile(S, 512)
    nt = S // tm
    hb = tm // POOL_HALO
    row = lambda b, i: (b * nt + i, 0)
    kern = functools.partial(_prompt_branch_kernel, tm=tm)
    return pl.pallas_call(
        kern,
        out_shape=(jax.ShapeDtypeStruct((T, POOL_W), BF16), jax.ShapeDtypeStruct((T, MEM_W), BF16)),
        grid=(B, nt),
        in_specs=[
            pl.BlockSpec((tm, POOL_W), row),
            pl.BlockSpec((POOL_HALO, POOL_W), lambda b, i: (jnp.maximum((b * nt + i) * hb - 1, 0), 0)),
            pl.BlockSpec((tm, MEM_W), row),
            pl.BlockSpec((n_mem, MEM_W), lambda b, i: (b, 0)),
            pl.BlockSpec((n_mem, MEM_W), lambda b, i: (b, 0)),
            pl.BlockSpec((len(POOL_WINDOWS), POOL_GC, POOL_GC), lambda b, i: (0, 0, 0)),
            pl.BlockSpec((1, POOL_W), lambda b, i: (0, 0)),
        ],
        out_specs=(pl.BlockSpec((tm, POOL_W), row), pl.BlockSpec((tm, MEM_W), row)),
        compiler_params=_cparams(("parallel", "arbitrary")),
        name="prompt_branch",
    )(u, u, qm, mk, mv, pw, ps)


def _sample_branch_kernel(ext_ref, qm_ref, mk_ref, mv_ref, pw_ref, ps_ref, op_ref, om_ref, *, nb):
    y = _pool_rows(ext_ref[...], lambda w: float(w), pw_ref, ps_ref)
    op_ref[...] = y.astype(op_ref.dtype)
    shape = (qm_ref.shape[1], mk_ref.shape[1])
    rj = lax.broadcasted_iota(jnp.int32, shape, 0)
    cj = lax.broadcasted_iota(jnp.int32, shape, 1)
    own = (cj & (MEM_HEADS - 1)) == (rj >> 2)
    for n in range(nb):
        s = jnp.where(own, _dot_nt(qm_ref[n], mk_ref[n].astype(BF16)) * MEM_SCALE, NEG_INF)
        e = jnp.exp(s - jnp.max(s, axis=-1, keepdims=True))
        p = e / jnp.sum(e, axis=-1, keepdims=True)
        om_ref[n] = _dot(p.astype(BF16), mv_ref[n].astype(BF16)).astype(om_ref.dtype)


def _sample_branch(ext, qm16, mk, mv, pw, ps, ext_rows):
    Bs, mem_rows, _ = mk.shape
    q_rows = qm16.shape[1]
    nb = _row_tile(Bs, 8)
    kern = functools.partial(_sample_branch_kernel, nb=nb)
    seq3 = lambda i: (i, 0, 0)
    return pl.pallas_call(
        kern,
        out_shape=(jax.ShapeDtypeStruct((Bs * ext_rows, POOL_W), BF16),
                   jax.ShapeDtypeStruct((Bs, q_rows, MEM_HEAD_DIM), BF16)),
        grid=(Bs // nb,),
        in_specs=[
            pl.BlockSpec((nb * ext_rows, POOL_W), lambda i: (i, 0)),
            pl.BlockSpec((nb, q_rows, MEM_HEAD_DIM), seq3),
            pl.BlockSpec((nb, mem_rows, MEM_HEAD_DIM), seq3),
            pl.BlockSpec((nb, mem_rows, MEM_HEAD_DIM), seq3),
            pl.BlockSpec((len(POOL_WINDOWS), POOL_GC, POOL_GC), lambda i: (0, 0, 0)),
            pl.BlockSpec((1, POOL_W), lambda i: (0, 0)),
        ],
        out_specs=(pl.BlockSpec((nb * ext_rows, POOL_W), lambda i: (i, 0)),
                   pl.BlockSpec((nb, q_rows, MEM_HEAD_DIM), seq3)),
        compiler_params=_cparams(("parallel",)),
        name="sample_branch",
    )(ext, qm16, mk, mv, pw, ps)


def _mem_kv_kernel(mem_ref, g_ref, w_ref, k_ref, v_ref):
    h = _rms(mem_ref[...], g_ref[...]).astype(BF16)
    kv = _dot(h, w_ref[...])
    k_ref[...] = kv[:, :MEM_W]
    v_ref[...] = kv[:, MEM_W:]


def _mem_kv(mem, g, w):
    R, D = mem.shape
    tm = _row_tile(R, 256)
    row = lambda i: (i, 0)
    const = lambda i: (0, 0)
    return pl.pallas_call(
        _mem_kv_kernel,
        out_shape=(jax.ShapeDtypeStruct((R, MEM_W), F32), jax.ShapeDtypeStruct((R, MEM_W), F32)),
        grid=(R // tm,),
        in_specs=[pl.BlockSpec((tm, D), row), pl.BlockSpec((1, D), const),
                  pl.BlockSpec((D, 2 * MEM_W), const)],
        out_specs=(pl.BlockSpec((tm, MEM_W), row), pl.BlockSpec((tm, MEM_W), row)),
        compiler_params=_cparams(("parallel",)),
        name="mem_kv",
    )(mem, g, w)


def _merge_kernel(x_ref, g1_ref, wg_ref, oa_ref, op_ref, om_ref, wa_ref, wp_ref, wm_ref, wo_ref, x1_ref):
    x = x_ref[...]
    d = x.shape[1]
    h = _rms(x, g1_ref[...]).astype(BF16)

    def gate(c):
        return jax.nn.sigmoid(_dot(h, wg_ref[:, c * d:(c + 1) * d]))

    mix = gate(0) * _dot(oa_ref[...], wa_ref[...])
    mix = mix + gate(1) * _dot(op_ref[...], wp_ref[...])
    mix = mix + gate(2) * _dot(om_ref[...], wm_ref[...])
    x1_ref[...] = x + _dot(mix.astype(BF16), wo_ref[...])


def _merge(x, g1, wg, oa, op, om, wa, wp, wm, wo):
    T, D = x.shape
    tm = _row_tile(T, 256)
    row = lambda i: (i, 0)
    const = lambda i: (0, 0)
    full = lambda a: pl.BlockSpec(a.shape, const)
    return pl.pallas_call(
        _merge_kernel,
        out_shape=jax.ShapeDtypeStruct((T, D), F32),
        grid=(T // tm,),
        in_specs=[pl.BlockSpec((tm, D), row), full(g1), full(wg),
                  pl.BlockSpec((tm, ATTN_W), row), pl.BlockSpec((tm, POOL_W), row),
                  pl.BlockSpec((tm, MEM_W), row), full(wa), full(wp), full(wm), full(wo)],
        out_specs=pl.BlockSpec((tm, D), row),
        compiler_params=_cparams(("parallel",)),
        name="merge",
    )(x, g1, wg, oa, op, om, wa, wp, wm, wo)


def _ffn_kernel(x1_ref, g2_ref, wg_ref, wu_ref, wd_ref, gf_ref, y_ref):
    x1 = x1_ref[...]
    h2 = _rms(x1, g2_ref[...]).astype(BF16)
    a = _dot(h2, wg_ref[...])
    b = _dot(h2, wu_ref[...])
    t = (jax.nn.silu(a) * b).astype(BF16)
    x2 = x1 + _dot(t, wd_ref[...])
    y_ref[...] = _rms(x2, gf_ref[...])


def _ffn(x1, g2, wg, wu, wd, gf):
    T, D = x1.shape
    tm = _row_tile(T, 256)
    row = lambda i: (i, 0)
    const = lambda i: (0, 0)
    full = lambda a: pl.BlockSpec(a.shape, const)
    return pl.pallas_call(
        _ffn_kernel,
        out_shape=jax.ShapeDtypeStruct((T, D), F32),
        grid=(T // tm,),
        in_specs=[pl.BlockSpec((tm, D), row), full(g2), full(wg), full(wu), full(wd), full(gf)],
        out_specs=pl.BlockSpec((tm, D), row),
        compiler_params=_cparams(("parallel",)),
        name="ffn",
    )(x1, g2, wg, wu, wd, gf)


def _rope_tables(pos):
    half = HEAD_DIM // 2
    inv = ROPE_THETA ** (-jnp.arange(half, dtype=F32) / half)
    ang = pos.astype(F32)[:, None] * inv[None, :]
    cos = jnp.cos(ang)
    sin = jnp.sin(ang)
    return jnp.tile(cos, (1, 4)), jnp.concatenate([-sin, sin, -sin, sin], axis=1)


def kernel(x_prompt, x_sample, cache_k, cache_v, page_table, state_pool, cache_mem_k, cache_mem_v,
           mem_prompt, norm1_g, w_in, lambda_q1, lambda_k1, lambda_q2, lambda_k2, subln_g,
           pool_w, pool_scale, norm_mem_g, w_mem_kv, w_attn_proj, w_pool_proj, w_mem_proj, w_out,
           norm2_g, w_ffn_gate, w_ffn_up, w_ffn_down, norm_f_g):
    B, S, D = x_prompt.shape
    Bs, Ss, _ = x_sample.shape
    assert Ss == 4 and D == ATTN_W
    depth = w_in.shape[0]
    assert depth == 1
    n_pool, page = cache_k.shape[1], cache_k.shape[2]
    past_len = page_table.shape[1] * page
    n_mem = mem_prompt.shape[1]
    row2 = lambda a: a.reshape(1, -1)

    cos_p, sin_p = _rope_tables(jnp.arange(S, dtype=jnp.int32))
    cos_s, sin_s = _rope_tables(past_len + jnp.arange(Ss, dtype=jnp.int32))
    cos_s = jnp.tile(cos_s, (Bs, 1))
    sin_s = jnp.tile(sin_s, (Bs, 1))

    l = 0
    lam_init = 0.8 - 0.6 * math.exp(-0.3 * l)
    n_qkvum = 3 * ATTN_W + POOL_W + MEM_W
    w_qkvum = w_in[l][:, :n_qkvum].astype(BF16)
    w_gate = w_in[l][:, n_qkvum:].astype(BF16)
    g1 = row2(norm1_g[l])
    lam_args = (row2(lambda_q1[l]), row2(lambda_k1[l]), row2(lambda_q2[l]), row2(lambda_k2[l]),
                row2(subln_g[l]))
    pw = pool_w[l].astype(BF16)
    ps = row2(pool_scale[l])
    dense_w = (w_attn_proj[l].astype(BF16), w_pool_proj[l].astype(BF16), w_mem_proj[l].astype(BF16),
               w_out[l].astype(BF16))
    ffn_w = (row2(norm2_g[l]), w_ffn_gate[l].astype(BF16), w_ffn_up[l].astype(BF16),
             w_ffn_down[l].astype(BF16), row2(norm_f_g))

    xp = x_prompt.reshape(B * S, D)
    mk_p, mv_p = _mem_kv(mem_prompt.reshape(B * n_mem, D), row2(norm_mem_g[l]), w_mem_kv[l].astype(BF16))
    q_p, kf_p, kb_p, vf_p, vb_p, u_p, qm_p = _in_proj(xp, g1, w_qkvum, cos_p, sin_p)
    oa_p = _prompt_attn(q_p, kb_p, vb_p, *lam_args, B, S, lam_init)
    op_p, om_p = _prompt_branch(u_p, qm_p, mk_p, mv_p, pw, ps, B, S)
    x1_p = _merge(xp, g1, w_gate, oa_p, op_p, om_p, *dense_w)
    y_p = _ffn(x1_p, *ffn_w)

    xs = x_sample.reshape(Bs * Ss, D)
    q_s, kf_s, kb_s, vf_s, vb_s, u_s, qm_s = _in_proj(xs, g1, w_qkvum, cos_s, sin_s)
    q_s3 = q_s.reshape(Bs, Ss, ATTN_W).astype(F32)
    q8 = jnp.concatenate([q_s3, q_s3], axis=1)
    head_rows = lambda a, n: a.reshape(n, -1, V_DIM)
    oa_s8 = _sample_attn(page_table, q8, head_rows(kb_s, Bs), head_rows(vb_s, Bs), *lam_args,
                         head_rows(cache_k[l], n_pool), head_rows(cache_v[l], n_pool), lam_init)
    oa_s = oa_s8[:, :Ss].reshape(Bs * Ss, ATTN_W).astype(BF16)
    ext_rows = 1 + POOL_BUF + Ss + 4
    u_s3 = u_s.reshape(Bs, Ss, POOL_W)
    ext = jnp.concatenate([jnp.zeros((Bs, 1, POOL_W), F32), state_pool[l], u_s3,
                           jnp.zeros((Bs, 4, POOL_W), F32)], axis=1).reshape(Bs * ext_rows, POOL_W)
    qm_hq = qm_s.reshape(Bs, Ss, MEM_HEADS, MEM_HEAD_DIM).transpose(0, 2, 1, 3)
    op_s_full, om_hq = _sample_branch(ext, qm_hq.reshape(Bs, MEM_HEADS * Ss, MEM_HEAD_DIM),
                                      cache_mem_k[l].reshape(Bs, n_mem * MEM_HEADS, MEM_HEAD_DIM),
                                      cache_mem_v[l].reshape(Bs, n_mem * MEM_HEADS, MEM_HEAD_DIM),
                                      pw, ps, ext_rows)
    op_s = op_s_full.reshape(Bs, ext_rows, POOL_W)[:, 1 + POOL_BUF:1 + POOL_BUF + Ss].reshape(Bs * Ss, POOL_W)
    om_s = om_hq.reshape(Bs, MEM_HEADS, Ss, MEM_HEAD_DIM).transpose(0, 2, 1, 3).reshape(Bs * Ss, MEM_W)
    x1_s = _merge(xs, g1, w_gate, oa_s, op_s, om_s, *dense_w)
    y_s = _ffn(x1_s, *ffn_w)

    pool_p = u_p.reshape(B, S, POOL_W)[:, S - POOL_BUF:]
    pool_s = jnp.concatenate([state_pool[l][:, Ss:], u_s3], axis=1)
    hshape = (N_HEADS, V_DIM)
    mshape = (MEM_HEADS, MEM_HEAD_DIM)
    return (y_p.reshape(B, S, D), y_s.reshape(Bs, Ss, D),
            kf_p.reshape(1, B, S, *hshape), vf_p.reshape(1, B, S, *hshape),
            pool_p[None],
            mk_p.reshape(1, B, n_mem, *mshape), mv_p.reshape(1, B, n_mem, *mshape),
            kf_s.reshape(1, Bs, Ss, *hshape), vf_s.reshape(1, Bs, Ss, *hshape),
            pool_s[None])
```

```python
import functools
import math

import jax
import jax.numpy as jnp
from jax import lax
from jax.experimental import pallas as pl
from jax.experimental.pallas import tpu as pltpu

F32 = jnp.float32
BF16 = jnp.bfloat16

N_HEADS = 8
HEAD_DIM = 64
V_DIM = 2 * HEAD_DIM
ATTN_W = N_HEADS * V_DIM
ROPE_THETA = 10000.0
POOL_WINDOWS = (2, 4, 8, 16)
POOL_GC = 128
POOL_W = len(POOL_WINDOWS) * POOL_GC
POOL_BUF = max(POOL_WINDOWS) - 1
POOL_HALO = 16
MEM_HEADS = 4
MEM_HEAD_DIM = 128
MEM_W = MEM_HEADS * MEM_HEAD_DIM
EPS = 1e-5
NEG_INF = -1e30
Q_SCALE = HEAD_DIM ** -0.5 * math.log2(math.e)
MEM_SCALE = MEM_HEAD_DIM ** -0.5

VMEM_LIMIT = 56 * 1024 * 1024


def _cparams(sem):
    return pltpu.CompilerParams(dimension_semantics=sem, vmem_limit_bytes=VMEM_LIMIT)


def _rms(x, g):
    return x * lax.rsqrt(jnp.mean(x * x, axis=-1, keepdims=True) + EPS) * g


def _dot(a, b):
    return jnp.dot(a, b, preferred_element_type=F32)


def _dot_nt(a, b):
    return lax.dot_general(a, b, (((1,), (1,)), ((), ())), preferred_element_type=F32)


def _row_tile(n, pref):
    t = min(pref, n)
    while n % t:
        t //= 2
    return t


def _in_proj_kernel(x_ref, g_ref, wq_ref, wk_ref, wv_ref, wum_ref, cos_ref, sin_ref,
                    q_ref, kf_ref, kb_ref, vf_ref, vt_ref, u_ref, qm_ref):
    h = _rms(x_ref[...], g_ref[...]).astype(BF16)
    cos = cos_ref[...]
    sin = sin_ref[...]
    lane = lax.broadcasted_iota(jnp.int32, cos.shape, 1)
    first_half = (lane & (HEAD_DIM - 1)) < HEAD_DIM // 2

    def rotary(zh):
        partner = jnp.where(first_half,
                            pltpu.roll(zh, V_DIM - HEAD_DIM // 2, axis=1),
                            pltpu.roll(zh, HEAD_DIM // 2, axis=1))
        return zh * cos + partner * sin

    zq = _dot(h, wq_ref[...])
    for hd in range(N_HEADS):
        sl = slice(hd * V_DIM, (hd + 1) * V_DIM)
        q_ref[:, sl] = (rotary(zq[:, sl]) * Q_SCALE).astype(BF16)
    zk = _dot(h, wk_ref[...])
    for hd in range(N_HEADS):
        sl = slice(hd * V_DIM, (hd + 1) * V_DIM)
        kr = rotary(zk[:, sl])
        kf_ref[:, sl] = kr
        kb_ref[:, sl] = kr.astype(BF16)
    zv = _dot(h, wv_ref[...])
    vf_ref[...] = zv
    for hd in range(N_HEADS):
        sl = slice(hd * V_DIM, (hd + 1) * V_DIM)
        vt_ref[sl, :] = zv[:, sl].T.astype(BF16)
    zu = _dot(h, wum_ref[...])
    u_ref[...] = zu[:, :POOL_W]
    qm_ref[...] = zu[:, POOL_W:].astype(BF16)


def _in_proj(x, g1, w, cos_t, sin_t):
    T, D = x.shape
    P = cos_t.shape[0]
    tm = _row_tile(math.gcd(T, P), 512)
    npos = P // tm
    assert POOL_W + MEM_W == ATTN_W
    row = lambda i: (i, 0)
    const = lambda i: (0, 0)
    posmap = lambda i: (i % npos, 0)
    outs = (
        jax.ShapeDtypeStruct((T, ATTN_W), BF16),
        jax.ShapeDtypeStruct((T, ATTN_W), F32),
        jax.ShapeDtypeStruct((T, ATTN_W), BF16),
        jax.ShapeDtypeStruct((T, ATTN_W), F32),
        jax.ShapeDtypeStruct((ATTN_W, T), BF16),
        jax.ShapeDtypeStruct((T, POOL_W), F32),
        jax.ShapeDtypeStruct((T, MEM_W), BF16),
    )
    return pl.pallas_call(
        _in_proj_kernel,
        out_shape=outs,
        grid=(T // tm,),
        in_specs=[
            pl.BlockSpec((tm, D), row),
            pl.BlockSpec((1, D), const),
            pl.BlockSpec((D, ATTN_W), lambda i: (0, 0)),
            pl.BlockSpec((D, ATTN_W), lambda i: (0, 1)),
            pl.BlockSpec((D, ATTN_W), lambda i: (0, 2)),
            pl.BlockSpec((D, ATTN_W), lambda i: (0, 3)),
            pl.BlockSpec((tm, V_DIM), posmap),
            pl.BlockSpec((tm, V_DIM), posmap),
        ],
        out_specs=(
            pl.BlockSpec((tm, ATTN_W), row),
            pl.BlockSpec((tm, ATTN_W), row),
            pl.BlockSpec((tm, ATTN_W), row),
            pl.BlockSpec((tm, ATTN_W), row),
            pl.BlockSpec((ATTN_W, tm), lambda i: (0, i)),
            pl.BlockSpec((tm, POOL_W), row),
            pl.BlockSpec((tm, MEM_W), row),
        ),
        compiler_params=_cparams(("parallel",)),
        name="in_proj",
    )(x, g1, w, w, w, w, cos_t, sin_t)


def _diff_lambda(lq1_ref, lk1_ref, lq2_ref, lk2_ref, lam_init):
    a = jnp.sum(lq1_ref[...] * lk1_ref[...], axis=-1, keepdims=True)
    b = jnp.sum(lq2_ref[...] * lk2_ref[...], axis=-1, keepdims=True)
    return jnp.exp(a) - jnp.exp(b) + lam_init


def _sub_norm(o, g, lam_init):
    return _rms(o, g) * (1.0 - lam_init)


ONES_ROWS = 16


def _prompt_attn_kernel(q_ref, k_ref, vt_ref, lq1_ref, lk1_ref, lq2_ref, lk2_ref, sg_ref, o_ref,
                        sta_ref, stb_ref, m_ref, acc_ref, *, tq, lam_init):
    qi = pl.program_id(2)
    q = q_ref[...]
    lane = lax.broadcasted_iota(jnp.int32, q.shape, 1)
    zero = jnp.zeros_like(q)
    qq = jnp.concatenate([jnp.where(lane < HEAD_DIM, q, zero),
                          jnp.where(lane >= HEAD_DIM, q, zero)], axis=0)
    ones = jnp.ones((ONES_ROWS, tq), BF16)

    def scores_to(ref, j):
        kb = k_ref[pl.ds(pl.multiple_of(j * tq, tq), tq), :]
        ref[...] = _dot_nt(kb, qq)

    def consume(j, ref, masked):
        vt = vt_ref[:, pl.ds(pl.multiple_of(j * tq, tq), tq)]
        st = ref[...]
        if masked:
            r = lax.broadcasted_iota(jnp.int32, st.shape, 0)
            c = lax.broadcasted_iota(jnp.int32, st.shape, 1)
            c = jnp.where(c >= tq, c - tq, c)
            st = jnp.where(r <= c, st, NEG_INF)
        m = m_ref[...]
        m_new = jnp.maximum(m, jnp.max(st, axis=0, keepdims=True))
        alpha = jnp.exp2(m - m_new)
        pt = jnp.exp2(st - m_new).astype(BF16)
        acc_ref[...] = alpha * acc_ref[...] + _dot(jnp.concatenate([vt, ones], axis=0), pt)
        m_ref[...] = m_new

    m_ref[...] = jnp.full(m_ref.shape, NEG_INF, F32)
    acc_ref[...] = jnp.zeros(acc_ref.shape, F32)
    scores_to(sta_ref, 0)

    def body(jj, carry):
        scores_to(stb_ref, 2 * jj + 1)
        consume(2 * jj, sta_ref, False)
        scores_to(sta_ref, 2 * jj + 2)
        consume(2 * jj + 1, stb_ref, False)
        return carry

    lax.fori_loop(0, qi // 2, body, 0)

    @pl.when(qi % 2 == 1)
    def _():
        scores_to(stb_ref, qi)
        consume(qi - 1, sta_ref, False)
        consume(qi, stb_ref, True)

    @pl.when(qi % 2 == 0)
    def _():
        consume(qi, sta_ref, True)

    acc = acc_ref[...]
    on = acc[:V_DIM] / acc[V_DIM:V_DIM + 1]
    lam = _diff_lambda(lq1_ref, lk1_ref, lq2_ref, lk2_ref, lam_init)
    o = (on[:, :tq] - lam * on[:, tq:]).T
    o_ref[...] = _sub_norm(o, sg_ref[...], lam_init).astype(o_ref.dtype)


def _prompt_attn(q, k, vt, lq1, lk1, lq2, lk2, sg, B, S, lam_init):
    T = q.shape[0]
    tq = _row_tile(S, 512)
    nq = S // tq
    vec = lambda b, h, i: (0, 0)
    kern = functools.partial(_prompt_attn_kernel, tq=tq, lam_init=lam_init)
    return pl.pallas_call(
        kern,
        out_shape=jax.ShapeDtypeStruct((T, ATTN_W), BF16),
        grid=(B, N_HEADS, nq),
        in_specs=[
            pl.BlockSpec((tq, V_DIM), lambda b, h, i: (b * nq + i, h)),
            pl.BlockSpec((S, V_DIM), lambda b, h, i: (b, h)),
            pl.BlockSpec((V_DIM, S), lambda b, h, i: (h, b)),
            pl.BlockSpec((1, HEAD_DIM), vec),
            pl.BlockSpec((1, HEAD_DIM), vec),
            pl.BlockSpec((1, HEAD_DIM), vec),
            pl.BlockSpec((1, HEAD_DIM), vec),
            pl.BlockSpec((1, V_DIM), vec),
        ],
        out_specs=pl.BlockSpec((tq, V_DIM), lambda b, h, i: (b * nq + i, h)),
        scratch_shapes=[pltpu.VMEM((tq, 2 * tq), F32), pltpu.VMEM((tq, 2 * tq), F32),
                        pltpu.VMEM((1, 2 * tq), F32), pltpu.VMEM((V_DIM + ONES_ROWS, 2 * tq), F32)],
        compiler_params=_cparams(("parallel", "parallel", "arbitrary")),
        name="prompt_attn",
    )(q, k, vt, lq1, lk1, lq2, lk2, sg)


N_GRP = 16
N_SUB = 4
HEAD_GRP = 4


def _sample_attn_kernel(pt_ref, q_ref, kn_ref, vn_ref, bias_ref, biasn_ref, lq1_ref, lk1_ref, lq2_ref, lk2_ref,
                        sg_ref, *rest, n_grp, lam_init):
    del pt_ref
    k_refs = rest[:n_grp]
    v_refs = rest[n_grp:2 * n_grp]
    o_ref = rest[2 * n_grp]
    qw_ref, qrows_ref, m_ref, l_ref, acc_ref = rest[2 * n_grp + 1:]
    g = pl.program_id(1)

    @pl.when(g == 0)
    def _():
        q8 = q_ref[0]
        r8 = lax.broadcasted_iota(jnp.int32, (8, V_DIM), 0)
        c8 = lax.broadcasted_iota(jnp.int32, (8, V_DIM), 1)
        own_map = (c8 >= HEAD_DIM) == (r8 >= 4)
        blocks = [jnp.where(own_map, q8[:, hd * V_DIM:(hd + 1) * V_DIM], 0.0) for hd in range(N_HEADS)]
        qrows_ref[...] = jnp.concatenate(blocks, axis=0).astype(BF16)
        zero = jnp.zeros_like(blocks[0])
        qw_ref[...] = jnp.concatenate(
            [jnp.concatenate([blocks[hd] if hd % HEAD_GRP == a else zero for a in range(HEAD_GRP)], axis=1)
             for hd in range(N_HEADS)], axis=0).astype(BF16)
        m_ref[...] = jnp.full(m_ref.shape, NEG_INF, F32)
        l_ref[...] = jnp.zeros(l_ref.shape, F32)
        acc_ref[...] = jnp.zeros(acc_ref.shape, F32)

    def view(ref):
        n = ref.shape[1] // HEAD_GRP
        return jnp.concatenate([ref[0, pl.ds(a, n, stride=HEAD_GRP), :].astype(BF16)
                                for a in range(HEAD_GRP)], axis=1)

    def update(m, l, acc, s, pv_fn):
        m_new = jnp.maximum(m, jnp.max(s, axis=-1, keepdims=True))
        alpha = jnp.exp2(m - m_new)
        p = jnp.exp2(s - m_new)
        l = alpha * l + jnp.sum(p, axis=-1, keepdims=True)
        acc = alpha * acc + pv_fn(p.astype(BF16))
        return m_new, l, acc

    qw = qw_ref[...]
    bias = bias_ref[...]
    n = bias.shape[1]
    n_sub = min(N_SUB, n_grp)
    sub = n_grp // n_sub
    scores = [jnp.concatenate([_dot_nt(qw, view(k_refs[u * sub + i])) + bias for i in range(sub)], axis=1)
              for u in range(n_sub)]
    m, l, acc = m_ref[...], l_ref[...], acc_ref[...]
    for u in range(n_sub):
        def past_pv(p, u=u):
            pv = _dot(p[:, 0:n], view(v_refs[u * sub]))
            for i in range(1, sub):
                pv = pv + _dot(p[:, i * n:(i + 1) * n], view(v_refs[u * sub + i]))
            return pv
        m, l, acc = update(m, l, acc, scores[u], past_pv)
    m_ref[...], l_ref[...], acc_ref[...] = m, l, acc

    @pl.when(g == pl.num_programs(1) - 1)
    def _():
        sn = _dot_nt(qrows_ref[...], kn_ref[0]) + biasn_ref[...]
        _, l2, acc2 = update(m, l, acc, sn,
                             lambda p: jnp.concatenate([_dot(p, vn_ref[0])] * HEAD_GRP, axis=1))
        lam = _diff_lambda(lq1_ref, lk1_ref, lq2_ref, lk2_ref, lam_init)
        on = acc2 / l2
        for hd in range(N_HEADS):
            a = hd % HEAD_GRP
            blk = on[hd * 8:(hd + 1) * 8, a * V_DIM:(a + 1) * V_DIM]
            o8 = blk - lam * pltpu.roll(blk, 4, axis=0)
            o_ref[0, :, hd * V_DIM:(hd + 1) * V_DIM] = _sub_norm(o8, sg_ref[...], lam_init)


def _sample_attn(page_table, q8, kn, vn, bias, biasn, lq1, lk1, lq2, lk2, sg, ck, cv, lam_init):
    Bs, n_pages = page_table.shape
    page_rows = ck.shape[1]
    new_rows = kn.shape[1]
    n_grp = N_GRP
    while n_pages % n_grp:
        n_grp //= 2
    vec = lambda b, g, pt: (0, 0)
    seq = lambda b, g, pt: (b, 0, 0)

    def page_spec(i):
        return pl.BlockSpec((1, page_rows, V_DIM), lambda b, g, pt: (pt[b, g * n_grp + i], 0, 0))

    kern = functools.partial(_sample_attn_kernel, n_grp=n_grp, lam_init=lam_init)
    grid_spec = pltpu.PrefetchScalarGridSpec(
        num_scalar_prefetch=1,
        grid=(Bs, n_pages // n_grp),
        in_specs=[
            pl.BlockSpec((1, 8, ATTN_W), seq),
            pl.BlockSpec((1, new_rows, V_DIM), seq),
            pl.BlockSpec((1, new_rows, V_DIM), seq),
            pl.BlockSpec(bias.shape, vec),
            pl.BlockSpec(biasn.shape, vec),
            pl.BlockSpec((1, HEAD_DIM), vec),
            pl.BlockSpec((1, HEAD_DIM), vec),
            pl.BlockSpec((1, HEAD_DIM), vec),
            pl.BlockSpec((1, HEAD_DIM), vec),
            pl.BlockSpec((1, V_DIM), vec),
        ] + [page_spec(i) for i in range(n_grp)] + [page_spec(i) for i in range(n_grp)],
        out_specs=pl.BlockSpec((1, 8, ATTN_W), seq),
        scratch_shapes=[
            pltpu.VMEM((N_HEADS * 8, HEAD_GRP * V_DIM), BF16),
            pltpu.VMEM((N_HEADS * 8, V_DIM), BF16),
            pltpu.VMEM((N_HEADS * 8, 1), F32),
            pltpu.VMEM((N_HEADS * 8, 1), F32),
            pltpu.VMEM((N_HEADS * 8, HEAD_GRP * V_DIM), F32),
        ],
    )
    return pl.pallas_call(
        kern,
        out_shape=jax.ShapeDtypeStruct((Bs, 8, ATTN_W), F32),
        grid_spec=grid_spec,
        compiler_params=_cparams(("parallel", "arbitrary")),
        name="sample_attn",
    )(page_table, q8, kn, vn, bias, biasn, lq1, lk1, lq2, lk2, sg, *([ck] * n_grp), *([cv] * n_grp))


def _page_bias(rows, cols):
    rj = jnp.arange(rows, dtype=jnp.int32)[:, None]
    cj = jnp.arange(cols, dtype=jnp.int32)[None, :]
    n_hg = N_HEADS // HEAD_GRP
    return jnp.where((cj % n_hg) == (rj // 8) // HEAD_GRP, 0.0, NEG_INF).astype(F32)


def _new_bias(rows, cols):
    rj = jnp.arange(rows, dtype=jnp.int32)[:, None]
    cj = jnp.arange(cols, dtype=jnp.int32)[None, :]
    keep = ((cj % N_HEADS) == (rj // 8)) & ((cj // N_HEADS) <= (rj % 4))
    return jnp.where(keep, 0.0, NEG_INF).astype(F32)


def _pool_rows(ext, cnt_fn, pw_ref, ps_ref):
    outs = []
    for gi, w in enumerate(POOL_WINDOWS):
        tok = ext[:, gi * POOL_GC:(gi + 1) * POOL_GC]
        cur = tok
        sh = 1
        while sh < w:
            cur = cur + pltpu.roll(cur, sh, axis=0)
            sh *= 2
        d = cur / cnt_fn(w) - tok
        outs.append(_dot(d.astype(BF16), pw_ref[gi]))
    return jnp.concatenate(outs, axis=-1) * ps_ref[...]


def _mem_attend(qm, mk, mv):
    outs = []
    for hd in range(MEM_HEADS):
        sl = slice(hd * MEM_HEAD_DIM, (hd + 1) * MEM_HEAD_DIM)
        s = _dot_nt(qm[:, sl], mk[:, sl]) * MEM_SCALE
        s = s - jnp.max(s, axis=-1, keepdims=True)
        e = jnp.exp(s)
        p = e / jnp.sum(e, axis=-1, keepdims=True)
        outs.append(_dot(p.astype(BF16), mv[:, sl]))
    return jnp.concatenate(outs, axis=-1)


def _prompt_branch_kernel(u_ref, halo_ref, qm_ref, mk_ref, mv_ref, pw_ref, ps_ref, op_ref, om_ref, *, tm):
    i = pl.program_id(1)
    halo = jnp.where(i > 0, halo_ref[...], 0.0)
    ext = jnp.concatenate([halo, u_ref[...]], axis=0)
    pos = i * tm - POOL_HALO + lax.broadcasted_iota(jnp.int32, (POOL_HALO + tm, 1), 0)
    cnt_fn = lambda w: jnp.clip(pos + 1, 1, w).astype(F32)
    y = _pool_rows(ext, cnt_fn, pw_ref, ps_ref)
    op_ref[...] = y[POOL_HALO:].astype(op_ref.dtype)
    om_ref[...] = _mem_attend(qm_ref[...], mk_ref[...].astype(BF16),
                              mv_ref[...].astype(BF16)).astype(om_ref.dtype)


def _prompt_branch(u, qm, mk, mv, pw, ps, B, S):
    T = u.shape[0]
    n_mem = mk.shape[0] // B
    tm = _row_tile(S, 512)
    nt = S // tm
    hb = tm // POOL_HALO
    row = lambda b, i: (b * nt + i, 0)
    kern = functools.partial(_prompt_branch_kernel, tm=tm)
    return pl.pallas_call(
        kern,
        out_shape=(jax.ShapeDtypeStruct((T, POOL_W), BF16), jax.ShapeDtypeStruct((T, MEM_W), BF16)),
        grid=(B, nt),
        in_specs=[
            pl.BlockSpec((tm, POOL_W), row),
            pl.BlockSpec((POOL_HALO, POOL_W), lambda b, i: (jnp.maximum((b * nt + i) * hb - 1, 0), 0)),
            pl.BlockSpec((tm, MEM_W), row),
            pl.BlockSpec((n_mem, MEM_W), lambda b, i: (b, 0)),
            pl.BlockSpec((n_mem, MEM_W), lambda b, i: (b, 0)),
            pl.BlockSpec((len(POOL_WINDOWS), POOL_GC, POOL_GC), lambda b, i: (0, 0, 0)),
            pl.BlockSpec((1, POOL_W), lambda b, i: (0, 0)),
        ],
        out_specs=(pl.BlockSpec((tm, POOL_W), row), pl.BlockSpec((tm, MEM_W), row)),
        compiler_params=_cparams(("parallel", "arbitrary")),
        name="prompt_branch",
    )(u, u, qm, mk, mv, pw, ps)


def _sample_branch_kernel(ext_ref, qm_ref, mk_ref, mv_ref, pw_ref, ps_ref, op_ref, om_ref, *, nb):
    y = _pool_rows(ext_ref[...], lambda w: float(w), pw_ref, ps_ref)
    op_ref[...] = y.astype(op_ref.dtype)
    shape = (qm_ref.shape[1], mk_ref.shape[1])
    rj = lax.broadcasted_iota(jnp.int32, shape, 0)
    cj = lax.broadcasted_iota(jnp.int32, shape, 1)
    own = (cj & (MEM_HEADS - 1)) == (rj >> 2)
    for n in range(nb):
        s = jnp.where(own, _dot_nt(qm_ref[n], mk_ref[n].astype(BF16)) * MEM_SCALE, NEG_INF)
        e = jnp.exp(s - jnp.max(s, axis=-1, keepdims=True))
        p = e / jnp.sum(e, axis=-1, keepdims=True)
        om_ref[n] = _dot(p.astype(BF16), mv_ref[n].astype(BF16)).astype(om_ref.dtype)


def _sample_branch(ext, qm16, mk, mv, pw, ps, ext_rows):
    Bs, mem_rows, _ = mk.shape
    q_rows = qm16.shape[1]
    nb = _row_tile(Bs, 8)
    kern = functools.partial(_sample_branch_kernel, nb=nb)
    seq3 = lambda i: (i, 0, 0)
    return pl.pallas_call(
        kern,
        out_shape=(jax.ShapeDtypeStruct((Bs * ext_rows, POOL_W), BF16),
                   jax.ShapeDtypeStruct((Bs, q_rows, MEM_HEAD_DIM), BF16)),
        grid=(Bs // nb,),
        in_specs=[
            pl.BlockSpec((nb * ext_rows, POOL_W), lambda i: (i, 0)),
            pl.BlockSpec((nb, q_rows, MEM_HEAD_DIM), seq3),
            pl.BlockSpec((nb, mem_rows, MEM_HEAD_DIM), seq3),
            pl.BlockSpec((nb, mem_rows, MEM_HEAD_DIM), seq3),
            pl.BlockSpec((len(POOL_WINDOWS), POOL_GC, POOL_GC), lambda i: (0, 0, 0)),
            pl.BlockSpec((1, POOL_W), lambda i: (0, 0)),
        ],
        out_specs=(pl.BlockSpec((nb * ext_rows, POOL_W), lambda i: (i, 0)),
                   pl.BlockSpec((nb, q_rows, MEM_HEAD_DIM), seq3)),
        compiler_params=_cparams(("parallel",)),
        name="sample_branch",
    )(ext, qm16, mk, mv, pw, ps)


def _mem_kv_kernel(mem_ref, g_ref, w_ref, k_ref, v_ref):
    h = _rms(mem_ref[...], g_ref[...]).astype(BF16)
    kv = _dot(h, w_ref[...])
    k_ref[...] = kv[:, :MEM_W]
    v_ref[...] = kv[:, MEM_W:]


def _mem_kv(mem, g, w):
    R, D = mem.shape
    tm = _row_tile(R, 256)
    row = lambda i: (i, 0)
    const = lambda i: (0, 0)
    return pl.pallas_call(
        _mem_kv_kernel,
        out_shape=(jax.ShapeDtypeStruct((R, MEM_W), F32), jax.ShapeDtypeStruct((R, MEM_W), F32)),
        grid=(R // tm,),
        in_specs=[pl.BlockSpec((tm, D), row), pl.BlockSpec((1, D), const),
                  pl.BlockSpec((D, 2 * MEM_W), const)],
        out_specs=(pl.BlockSpec((tm, MEM_W), row), pl.BlockSpec((tm, MEM_W), row)),
        compiler_params=_cparams(("parallel",)),
        name="mem_kv",
    )(mem, g, w)


def _merge_kernel(x_ref, g1_ref, wga_ref, wgp_ref, wgm_ref, oa_ref, op_ref, om_ref,
                  wa_ref, wp_ref, wm_ref, wo_ref, x1_ref):
    x = x_ref[...]
    h = _rms(x, g1_ref[...]).astype(BF16)

    def gate(wg_ref):
        return jax.nn.sigmoid(_dot(h, wg_ref[...]))

    mix = gate(wga_ref) * _dot(oa_ref[...], wa_ref[...])
    mix = mix + gate(wgp_ref) * _dot(op_ref[...], wp_ref[...])
    mix = mix + gate(wgm_ref) * _dot(om_ref[...], wm_ref[...])
    x1_ref[...] = x + _dot(mix.astype(BF16), wo_ref[...])


def _merge(x, g1, w_in, oa, op, om, wa, wp, wm, wo):
    T, D = x.shape
    tm = _row_tile(T, 512)
    gate0 = w_in.shape[1] // D - 3
    row = lambda i: (i, 0)
    const = lambda i: (0, 0)
    full = lambda a: pl.BlockSpec(a.shape, const)
    return pl.pallas_call(
        _merge_kernel,
        out_shape=jax.ShapeDtypeStruct((T, D), F32),
        grid=(T // tm,),
        in_specs=[pl.BlockSpec((tm, D), row), full(g1),
                  pl.BlockSpec((D, D), lambda i: (0, gate0)),
                  pl.BlockSpec((D, D), lambda i: (0, gate0 + 1)),
                  pl.BlockSpec((D, D), lambda i: (0, gate0 + 2)),
                  pl.BlockSpec((tm, ATTN_W), row), pl.BlockSpec((tm, POOL_W), row),
                  pl.BlockSpec((tm, MEM_W), row), full(wa), full(wp), full(wm), full(wo)],
        out_specs=pl.BlockSpec((tm, D), row),
        compiler_params=_cparams(("parallel",)),
        name="merge",
    )(x, g1, w_in, w_in, w_in, oa, op, om, wa, wp, wm, wo)


def _ffn_kernel(x1_ref, g2_ref, wg_ref, wu_ref, wd_ref, gf_ref, y_ref):
    x1 = x1_ref[...]
    h2 = _rms(x1, g2_ref[...]).astype(BF16)
    a = _dot(h2, wg_ref[...])
    b = _dot(h2, wu_ref[...])
    t = (jax.nn.silu(a) * b).astype(BF16)
    x2 = x1 + _dot(t, wd_ref[...])
    y_ref[...] = _rms(x2, gf_ref[...])


def _ffn(x1, g2, wg, wu, wd, gf):
    T, D = x1.shape
    tm = _row_tile(T, 256)
    row = lambda i: (i, 0)
    const = lambda i: (0, 0)
    full = lambda a: pl.BlockSpec(a.shape, const)
    return pl.pallas_call(
        _ffn_kernel,
        out_shape=jax.ShapeDtypeStruct((T, D), F32),
        grid=(T // tm,),
        in_specs=[pl.BlockSpec((tm, D), row), full(g2), full(wg), full(wu), full(wd), full(gf)],
        out_specs=pl.BlockSpec((tm, D), row),
        compiler_params=_cparams(("parallel",)),
        name="ffn",
    )(x1, g2, wg, wu, wd, gf)


def _rope_tables(pos):
    half = HEAD_DIM // 2
    inv = ROPE_THETA ** (-jnp.arange(half, dtype=F32) / half)
    ang = pos.astype(F32)[:, None] * inv[None, :]
    cos = jnp.cos(ang)
    sin = jnp.sin(ang)
    return jnp.tile(cos, (1, 4)), jnp.concatenate([-sin, sin, -sin, sin], axis=1)


def kernel(x_prompt, x_sample, cache_k, cache_v, page_table, state_pool, cache_mem_k, cache_mem_v,
           mem_prompt, norm1_g, w_in, lambda_q1, lambda_k1, lambda_q2, lambda_k2, subln_g,
           pool_w, pool_scale, norm_mem_g, w_mem_kv, w_attn_proj, w_pool_proj, w_mem_proj, w_out,
           norm2_g, w_ffn_gate, w_ffn_up, w_ffn_down, norm_f_g):
    B, S, D = x_prompt.shape
    Bs, Ss, _ = x_sample.shape
    assert Ss == 4 and D == ATTN_W
    depth = w_in.shape[0]
    assert depth == 1
    n_pool, page = cache_k.shape[1], cache_k.shape[2]
    past_len = page_table.shape[1] * page
    n_mem = mem_prompt.shape[1]
    row2 = lambda a: a.reshape(1, -1)

    cos_p, sin_p = _rope_tables(jnp.arange(S, dtype=jnp.int32))
    cos_s, sin_s = _rope_tables(past_len + jnp.arange(Ss, dtype=jnp.int32))
    cos_s = jnp.tile(cos_s, (Bs, 1))
    sin_s = jnp.tile(sin_s, (Bs, 1))

    l = 0
    lam_init = 0.8 - 0.6 * math.exp(-0.3 * l)
    w_in_b = w_in[l].astype(BF16)
    g1 = row2(norm1_g[l])
    lam_args = (row2(lambda_q1[l]), row2(lambda_k1[l]), row2(lambda_q2[l]), row2(lambda_k2[l]),
                row2(subln_g[l]))
    pw = pool_w[l].astype(BF16)
    ps = row2(pool_scale[l])
    dense_w = (w_attn_proj[l].astype(BF16), w_pool_proj[l].astype(BF16), w_mem_proj[l].astype(BF16),
               w_out[l].astype(BF16))
    ffn_w = (row2(norm2_g[l]), w_ffn_gate[l].astype(BF16), w_ffn_up[l].astype(BF16),
             w_ffn_down[l].astype(BF16), row2(norm_f_g))

    xp = x_prompt.reshape(B * S, D)
    mk_p, mv_p = _mem_kv(mem_prompt.reshape(B * n_mem, D), row2(norm_mem_g[l]), w_mem_kv[l].astype(BF16))
    q_p, kf_p, kb_p, vf_p, vt_p, u_p, qm_p = _in_proj(xp, g1, w_in_b, cos_p, sin_p)
    oa_p = _prompt_attn(q_p, kb_p, vt_p, *lam_args, B, S, lam_init)
    op_p, om_p = _prompt_branch(u_p, qm_p, mk_p, mv_p, pw, ps, B, S)
    x1_p = _merge(xp, g1, w_in_b, oa_p, op_p, om_p, *dense_w)
    y_p = _ffn(x1_p, *ffn_w)

    xs = x_sample.reshape(Bs * Ss, D)
    q_s, kf_s, kb_s, vf_s, _, u_s, qm_s = _in_proj(xs, g1, w_in_b, cos_s, sin_s)
    q_s3 = q_s.reshape(Bs, Ss, ATTN_W).astype(F32)
    q8 = jnp.concatenate([q_s3, q_s3], axis=1)
    head_rows = lambda a, n: a.reshape(n, -1, V_DIM)
    score_rows = N_HEADS * 8
    oa_s8 = _sample_attn(page_table, q8, head_rows(kb_s, Bs), head_rows(vf_s.astype(BF16), Bs),
                         _page_bias(score_rows, page * N_HEADS // HEAD_GRP), _new_bias(score_rows, Ss * N_HEADS),
                         *lam_args, head_rows(cache_k[l], n_pool), head_rows(cache_v[l], n_pool), lam_init)
    oa_s = oa_s8[:, :Ss].reshape(Bs * Ss, ATTN_W).astype(BF16)
    ext_rows = 1 + POOL_BUF + Ss + 4
    u_s3 = u_s.reshape(Bs, Ss, POOL_W)
    ext = jnp.concatenate([jnp.zeros((Bs, 1, POOL_W), F32), state_pool[l], u_s3,
                           jnp.zeros((Bs, 4, POOL_W), F32)], axis=1).reshape(Bs * ext_rows, POOL_W)
    qm_hq = qm_s.reshape(Bs, Ss, MEM_HEADS, MEM_HEAD_DIM).transpose(0, 2, 1, 3)
    op_s_full, om_hq = _sample_branch(ext, qm_hq.reshape(Bs, MEM_HEADS * Ss, MEM_HEAD_DIM),
                                      cache_mem_k[l].reshape(Bs, n_mem * MEM_HEADS, MEM_HEAD_DIM),
                                      cache_mem_v[l].reshape(Bs, n_mem * MEM_HEADS, MEM_HEAD_DIM),
                                      pw, ps, ext_rows)
    op_s = op_s_full.reshape(Bs, ext_rows, POOL_W)[:, 1 + POOL_BUF:1 + POOL_BUF + Ss].reshape(Bs * Ss, POOL_W)
    om_s = om_hq.reshape(Bs, MEM_HEADS, Ss, MEM_HEAD_DIM).transpose(0, 2, 1, 3).reshape(Bs * Ss, MEM_W)
    x1_s = _merge(xs, g1, w_in_b, oa_s, op_s, om_s, *dense_w)
    y_s = _ffn(x1_s, *ffn_w)

    pool_p = u_p.reshape(B, S, POOL_W)[:, S - POOL_BUF:]
    pool_s = jnp.concatenate([state_pool[l][:, Ss:], u_s3], axis=1)
    hshape = (N_HEADS, V_DIM)
    mshape = (MEM_HEADS, MEM_HEAD_DIM)
    return (y_p.reshape(B, S, D), y_s.reshape(Bs, Ss, D),
            kf_p.reshape(1, B, S, *hshape), vf_p.reshape(1, B, S, *hshape),
            pool_p[None],
            mk_p.reshape(1, B, n_mem, *mshape), mv_p.reshape(1, B, n_mem, *mshape),
            kf_s.reshape(1, Bs, Ss, *hshape), vf_s.reshape(1, Bs, Ss, *hshape),
            pool_s[None])
```

```python
import functools
import math

import jax
import jax.numpy as jnp
import numpy as np
from jax import lax
from jax.experimental import pallas as pl
from jax.experimental.pallas import tpu as pltpu

F32 = jnp.float32
BF16 = jnp.bfloat16

N_HEADS = 8
HEAD_DIM = 64
V_DIM = 2 * HEAD_DIM
ATTN_W = N_HEADS * V_DIM
ROPE_THETA = 10000.0
POOL_WINDOWS = (2, 4, 8, 16)
POOL_GC = 128
POOL_W = len(POOL_WINDOWS) * POOL_GC
POOL_BUF = max(POOL_WINDOWS) - 1
POOL_HALO = 16
MEM_HEADS = 4
MEM_HEAD_DIM = 128
MEM_W = MEM_HEADS * MEM_HEAD_DIM
EPS = 1e-5
NEG_INF = -1e30
Q_SCALE = HEAD_DIM ** -0.5 * math.log2(math.e)
MEM_SCALE = MEM_HEAD_DIM ** -0.5

VMEM_LIMIT = 56 * 1024 * 1024


def _cparams(sem):
    return pltpu.CompilerParams(dimension_semantics=sem, vmem_limit_bytes=VMEM_LIMIT)


def _rms(x, g):
    return x * lax.rsqrt(jnp.mean(x * x, axis=-1, keepdims=True) + EPS) * g


def _dot(a, b):
    return jnp.dot(a, b, preferred_element_type=F32)


def _dot_nt(a, b):
    return lax.dot_general(a, b, (((1,), (1,)), ((), ())), preferred_element_type=F32)


def _row_tile(n, pref):
    t = min(pref, n)
    while n % t:
        t //= 2
    return t


def _in_proj_kernel(x_ref, g_ref, wq_ref, wk_ref, wv_ref, wum_ref, cos_ref, sin_ref,
                    q_ref, kf_ref, kb_ref, vf_ref, vt_ref, u_ref, qm_ref):
    h = _rms(x_ref[...], g_ref[...]).astype(BF16)
    cos = cos_ref[...]
    sin = sin_ref[...]
    lane = lax.broadcasted_iota(jnp.int32, cos.shape, 1)
    first_half = (lane & (HEAD_DIM - 1)) < HEAD_DIM // 2

    def rotary(zh):
        partner = jnp.where(first_half,
                            pltpu.roll(zh, V_DIM - HEAD_DIM // 2, axis=1),
                            pltpu.roll(zh, HEAD_DIM // 2, axis=1))
        return zh * cos + partner * sin

    zq = _dot(h, wq_ref[...])
    for hd in range(N_HEADS):
        sl = slice(hd * V_DIM, (hd + 1) * V_DIM)
        q_ref[:, sl] = (rotary(zq[:, sl]) * Q_SCALE).astype(BF16)
    zk = _dot(h, wk_ref[...])
    for hd in range(N_HEADS):
        sl = slice(hd * V_DIM, (hd + 1) * V_DIM)
        kr = rotary(zk[:, sl])
        kf_ref[:, sl] = kr
        kb_ref[:, sl] = kr.astype(BF16)
    zv = _dot(h, wv_ref[...])
    vf_ref[...] = zv
    for hd in range(N_HEADS):
        sl = slice(hd * V_DIM, (hd + 1) * V_DIM)
        vt_ref[sl, :] = zv[:, sl].T.astype(BF16)
    zu = _dot(h, wum_ref[...])
    u_ref[...] = zu[:, :POOL_W]
    qm_ref[...] = zu[:, POOL_W:].astype(BF16)


def _in_proj(x, g1, w, cos_t, sin_t):
    T, D = x.shape
    P = cos_t.shape[0]
    tm = _row_tile(math.gcd(T, P), 512)
    npos = P // tm
    assert POOL_W + MEM_W == ATTN_W
    row = lambda i: (i, 0)
    const = lambda i: (0, 0)
    posmap = lambda i: (i % npos, 0)
    outs = (
        jax.ShapeDtypeStruct((T, ATTN_W), BF16),
        jax.ShapeDtypeStruct((T, ATTN_W), F32),
        jax.ShapeDtypeStruct((T, ATTN_W), BF16),
        jax.ShapeDtypeStruct((T, ATTN_W), F32),
        jax.ShapeDtypeStruct((ATTN_W, T), BF16),
        jax.ShapeDtypeStruct((T, POOL_W), F32),
        jax.ShapeDtypeStruct((T, MEM_W), BF16),
    )
    return pl.pallas_call(
        _in_proj_kernel,
        out_shape=outs,
        grid=(T // tm,),
        in_specs=[
            pl.BlockSpec((tm, D), row),
            pl.BlockSpec((1, D), const),
            pl.BlockSpec((D, ATTN_W), lambda i: (0, 0)),
            pl.BlockSpec((D, ATTN_W), lambda i: (0, 1)),
            pl.BlockSpec((D, ATTN_W), lambda i: (0, 2)),
            pl.BlockSpec((D, ATTN_W), lambda i: (0, 3)),
            pl.BlockSpec((tm, V_DIM), posmap),
            pl.BlockSpec((tm, V_DIM), posmap),
        ],
        out_specs=(
            pl.BlockSpec((tm, ATTN_W), row),
            pl.BlockSpec((tm, ATTN_W), row),
            pl.BlockSpec((tm, ATTN_W), row),
            pl.BlockSpec((tm, ATTN_W), row),
            pl.BlockSpec((ATTN_W, tm), lambda i: (0, i)),
            pl.BlockSpec((tm, POOL_W), row),
            pl.BlockSpec((tm, MEM_W), row),
        ),
        compiler_params=_cparams(("parallel",)),
        name="in_proj",
    )(x, g1, w, w, w, w, cos_t, sin_t)


def _diff_lambda(lq1_ref, lk1_ref, lq2_ref, lk2_ref, lam_init):
    a = jnp.sum(lq1_ref[...] * lk1_ref[...], axis=-1, keepdims=True)
    b = jnp.sum(lq2_ref[...] * lk2_ref[...], axis=-1, keepdims=True)
    return jnp.exp(a) - jnp.exp(b) + lam_init


def _sub_norm(o, g, lam_init):
    return _rms(o, g) * (1.0 - lam_init)


ONES_ROWS = 16
N_GRP = 16
N_SUB = 4
HEAD_GRP = 4
P_SLOTS = 3
(S_QBLK, S_HEAD, S_NVALID, S_FIRST, S_LAST, S_KBLK) = range(6)


def _prompt_schedule(B, nq, n_steps):
    rows = []
    for b in range(B):
        for h in range(N_HEADS):
            for qi in range(nq):
                blocks = [qi] + list(range(qi))
                n_chunks = -(-len(blocks) // P_SLOTS)
                bounds = [len(blocks) * c // n_chunks for c in range(n_chunks + 1)]
                for c in range(n_chunks):
                    chunk = blocks[bounds[c]:bounds[c + 1]]
                    kb = [b * nq + j for j in chunk]
                    kb += [kb[-1]] * (P_SLOTS - len(chunk))
                    rows.append([b * nq + qi, h, len(chunk), int(c == 0), int(c == n_chunks - 1)] + kb)
    assert len(rows) <= n_steps, (len(rows), n_steps)
    idle = rows[-1][:2] + [0, 0, 0] + rows[-1][S_KBLK:]
    rows += [idle] * (n_steps - len(rows))
    return np.asarray(rows, np.int32).T


def _fused_attn_kernel(pt_ref, sched_ref, q_ref, kn_ref, vn_ref, bias_ref, biasn_ref,
                       lq1_ref, lk1_ref, lq2_ref, lk2_ref, sg_ref, pq_ref, *rest, n_grp, tq, lam_init):
    del pt_ref
    pk_refs = rest[:P_SLOTS]
    pvt_refs = rest[P_SLOTS:2 * P_SLOTS]
    rest = rest[2 * P_SLOTS:]
    k_refs = rest[:n_grp]
    v_refs = rest[n_grp:2 * n_grp]
    o_ref, po_ref = rest[2 * n_grp:2 * n_grp + 2]
    qw_ref, qrows_ref, m_ref, l_ref, acc_ref, pm_ref, pacc_ref = rest[2 * n_grp + 2:2 * n_grp + 9]
    st_refs = rest[2 * n_grp + 9:]
    g = pl.program_id(1)
    step = pl.program_id(0) * pl.num_programs(1) + g


    @pl.when(g == 0)
    def _():
        q8 = q_ref[0]
        r8 = lax.broadcasted_iota(jnp.int32, (8, V_DIM), 0)
        c8 = lax.broadcasted_iota(jnp.int32, (8, V_DIM), 1)
        own_map = (c8 >= HEAD_DIM) == (r8 >= 4)
        blocks = [jnp.where(own_map, q8[:, hd * V_DIM:(hd + 1) * V_DIM], 0.0) for hd in range(N_HEADS)]
        qrows_ref[...] = jnp.concatenate(blocks, axis=0).astype(BF16)
        zero = jnp.zeros_like(blocks[0])
        qw_ref[...] = jnp.concatenate(
            [jnp.concatenate([blocks[hd] if hd % HEAD_GRP == a else zero for a in range(HEAD_GRP)], axis=1)
             for hd in range(N_HEADS)], axis=0).astype(BF16)
        m_ref[...] = jnp.full(m_ref.shape, NEG_INF, F32)
        l_ref[...] = jnp.zeros(l_ref.shape, F32)
        acc_ref[...] = jnp.zeros(acc_ref.shape, F32)

    def view(ref):
        n = ref.shape[1] // HEAD_GRP
        return jnp.concatenate([ref[0, pl.ds(a, n, stride=HEAD_GRP), :].astype(BF16)
                                for a in range(HEAD_GRP)], axis=1)

    def update(m, l, acc, s, pv_fn):
        m_new = jnp.maximum(m, jnp.max(s, axis=-1, keepdims=True))
        alpha = jnp.exp2(m - m_new)
        p = jnp.exp2(s - m_new)
        l = alpha * l + jnp.sum(p, axis=-1, keepdims=True)
        acc = alpha * acc + pv_fn(p.astype(BF16))
        return m_new, l, acc

    qw = qw_ref[...]
    bias = bias_ref[...]
    pq = pq_ref[...]
    lane = lax.broadcasted_iota(jnp.int32, pq.shape, 1)
    pzero = jnp.zeros_like(pq)
    qq = jnp.concatenate([jnp.where(lane < HEAD_DIM, pq, pzero),
                          jnp.where(lane >= HEAD_DIM, pq, pzero)], axis=0)
    n = bias.shape[1]
    n_sub = min(N_SUB, n_grp)
    sub = n_grp // n_sub
    scores = []
    for u in range(max(n_sub, P_SLOTS)):
        if u < n_sub:
            scores.append(jnp.concatenate(
                [_dot_nt(qw, view(k_refs[u * sub + i])) + bias for i in range(sub)], axis=1))
        if u < P_SLOTS:
            st_refs[u][...] = _dot_nt(pk_refs[u][...], qq)
    m, l, acc = m_ref[...], l_ref[...], acc_ref[...]
    for u in range(n_sub):
        def past_pv(p, u=u):
            pv = _dot(p[:, 0:n], view(v_refs[u * sub]))
            for i in range(1, sub):
                pv = pv + _dot(p[:, i * n:(i + 1) * n], view(v_refs[u * sub + i]))
            return pv
        m, l, acc = update(m, l, acc, scores[u], past_pv)
    m_ref[...], l_ref[...], acc_ref[...] = m, l, acc

    @pl.when(g == pl.num_programs(1) - 1)
    def _():
        sn = _dot_nt(qrows_ref[...], kn_ref[0]) + biasn_ref[...]
        _, l2, acc2 = update(m, l, acc, sn,
                             lambda p: jnp.concatenate([_dot(p, vn_ref[0])] * HEAD_GRP, axis=1))
        lam = _diff_lambda(lq1_ref, lk1_ref, lq2_ref, lk2_ref, lam_init)
        on = acc2 / l2
        for hd in range(N_HEADS):
            a = hd % HEAD_GRP
            blk = on[hd * 8:(hd + 1) * 8, a * V_DIM:(a + 1) * V_DIM]
            o8 = blk - lam * pltpu.roll(blk, 4, axis=0)
            o_ref[0, :, hd * V_DIM:(hd + 1) * V_DIM] = _sub_norm(o8, sg_ref[...], lam_init)

    nvalid = sched_ref[S_NVALID, step]
    first = sched_ref[S_FIRST, step]
    last = sched_ref[S_LAST, step]

    def consume(slot, masked):
        st = st_refs[slot][...]
        if masked:
            r = lax.broadcasted_iota(jnp.int32, st.shape, 0)
            c = lax.broadcasted_iota(jnp.int32, st.shape, 1)
            c = jnp.where(c >= tq, c - tq, c)
            st = jnp.where(r <= c, st, NEG_INF)
        pm = pm_ref[...]
        pm_new = jnp.maximum(pm, jnp.max(st, axis=0, keepdims=True))
        alpha = jnp.exp2(pm - pm_new)
        pt = jnp.exp2(st - pm_new).astype(BF16)
        vt1 = jnp.concatenate([pvt_refs[slot][...], jnp.ones((ONES_ROWS, tq), BF16)], axis=0)
        pacc_ref[...] = alpha * pacc_ref[...] + _dot(vt1, pt)
        pm_ref[...] = pm_new

    @pl.when(first == 1)
    def _():
        pm_ref[...] = jnp.full(pm_ref.shape, NEG_INF, F32)
        pacc_ref[...] = jnp.zeros(pacc_ref.shape, F32)
        consume(0, True)

    @pl.when((first == 0) & (nvalid > 0))
    def _():
        consume(0, False)

    for slot in range(1, P_SLOTS):
        @pl.when(nvalid > slot)
        def _(slot=slot):
            consume(slot, False)

    @pl.when(last == 1)
    def _():
        pacc = pacc_ref[...]
        on = pacc[:V_DIM] / pacc[V_DIM:V_DIM + 1]
        lam = _diff_lambda(lq1_ref, lk1_ref, lq2_ref, lk2_ref, lam_init)
        o = (on[:, :tq] - lam * on[:, tq:]).T
        po_ref[...] = _sub_norm(o, sg_ref[...], lam_init).astype(po_ref.dtype)


def _fused_attn(page_table, q8, kn, vn, bias, biasn, lq1, lk1, lq2, lk2, sg, ck, cv,
                pq, pk, pvt, B, S, lam_init):
    Bs, n_pages = page_table.shape
    page_rows = ck.shape[1]
    new_rows = kn.shape[1]
    n_grp = N_GRP
    while n_pages % n_grp:
        n_grp //= 2
    n_groups = n_pages // n_grp
    T = pq.shape[0]
    tq = _row_tile(S, 512)
    sched = jnp.asarray(_prompt_schedule(B, S // tq, Bs * n_groups))
    step = lambda b, g: b * n_groups + g
    vec = lambda b, g, pt, sc: (0, 0)
    seq = lambda b, g, pt, sc: (b, 0, 0)
    qmap = lambda b, g, pt, sc: (sc[S_QBLK, step(b, g)], sc[S_HEAD, step(b, g)])

    def page_spec(i):
        return pl.BlockSpec((1, page_rows, V_DIM), lambda b, g, pt, sc: (pt[b, g * n_grp + i], 0, 0))

    def pk_spec(i):
        return pl.BlockSpec((tq, V_DIM), lambda b, g, pt, sc: (sc[S_KBLK + i, step(b, g)], sc[S_HEAD, step(b, g)]))

    def pvt_spec(i):
        return pl.BlockSpec((V_DIM, tq), lambda b, g, pt, sc: (sc[S_HEAD, step(b, g)], sc[S_KBLK + i, step(b, g)]))

    kern = functools.partial(_fused_attn_kernel, n_grp=n_grp, tq=tq, lam_init=lam_init)
    grid_spec = pltpu.PrefetchScalarGridSpec(
        num_scalar_prefetch=2,
        grid=(Bs, n_groups),
        in_specs=[
            pl.BlockSpec((1, 8, ATTN_W), seq),
            pl.BlockSpec((1, new_rows, V_DIM), seq),
            pl.BlockSpec((1, new_rows, V_DIM), seq),
            pl.BlockSpec(bias.shape, vec),
            pl.BlockSpec(biasn.shape, vec),
            pl.BlockSpec((1, HEAD_DIM), vec),
            pl.BlockSpec((1, HEAD_DIM), vec),
            pl.BlockSpec((1, HEAD_DIM), vec),
            pl.BlockSpec((1, HEAD_DIM), vec),
            pl.BlockSpec((1, V_DIM), vec),
            pl.BlockSpec((tq, V_DIM), qmap),
        ] + [pk_spec(i) for i in range(P_SLOTS)] + [pvt_spec(i) for i in range(P_SLOTS)]
          + [page_spec(i) for i in range(n_grp)] + [page_spec(i) for i in range(n_grp)],
        out_specs=(pl.BlockSpec((1, 8, ATTN_W), seq), pl.BlockSpec((tq, V_DIM), qmap)),
        scratch_shapes=[
            pltpu.VMEM((N_HEADS * 8, HEAD_GRP * V_DIM), BF16),
            pltpu.VMEM((N_HEADS * 8, V_DIM), BF16),
            pltpu.VMEM((N_HEADS * 8, 1), F32),
            pltpu.VMEM((N_HEADS * 8, 1), F32),
            pltpu.VMEM((N_HEADS * 8, HEAD_GRP * V_DIM), F32),
            pltpu.VMEM((1, 2 * tq), F32),
            pltpu.VMEM((V_DIM + ONES_ROWS, 2 * tq), F32),
        ] + [pltpu.VMEM((tq, 2 * tq), F32) for _ in range(P_SLOTS)],
    )
    return pl.pallas_call(
        kern,
        out_shape=(jax.ShapeDtypeStruct((Bs, 8, ATTN_W), F32), jax.ShapeDtypeStruct((T, ATTN_W), BF16)),
        grid_spec=grid_spec,
        compiler_params=_cparams(("arbitrary", "arbitrary")),
        name="fused_attn",
    )(page_table, sched, q8, kn, vn, bias, biasn, lq1, lk1, lq2, lk2, sg, pq,
      *([pk] * P_SLOTS), *([pvt] * P_SLOTS), *([ck] * n_grp), *([cv] * n_grp))


def _page_bias(rows, cols):
    rj = jnp.arange(rows, dtype=jnp.int32)[:, None]
    cj = jnp.arange(cols, dtype=jnp.int32)[None, :]
    n_hg = N_HEADS // HEAD_GRP
    return jnp.where((cj % n_hg) == (rj // 8) // HEAD_GRP, 0.0, NEG_INF).astype(F32)


def _new_bias(rows, cols):
    rj = jnp.arange(rows, dtype=jnp.int32)[:, None]
    cj = jnp.arange(cols, dtype=jnp.int32)[None, :]
    keep = ((cj % N_HEADS) == (rj // 8)) & ((cj // N_HEADS) <= (rj % 4))
    return jnp.where(keep, 0.0, NEG_INF).astype(F32)


def _pool_rows(ext, cnt_fn, pw_ref, ps_ref):
    outs = []
    for gi, w in enumerate(POOL_WINDOWS):
        tok = ext[:, gi * POOL_GC:(gi + 1) * POOL_GC]
        cur = tok
        sh = 1
        while sh < w:
            cur = cur + pltpu.roll(cur, sh, axis=0)
            sh *= 2
        d = cur / cnt_fn(w) - tok
        outs.append(_dot(d.astype(BF16), pw_ref[gi]))
    return jnp.concatenate(outs, axis=-1) * ps_ref[...]


def _mem_attend(qm, mk, mv):
    outs = []
    for hd in range(MEM_HEADS):
        sl = slice(hd * MEM_HEAD_DIM, (hd + 1) * MEM_HEAD_DIM)
        s = _dot_nt(qm[:, sl], mk[:, sl]) * MEM_SCALE
        s = s - jnp.max(s, axis=-1, keepdims=True)
        e = jnp.exp(s)
        p = e / jnp.sum(e, axis=-1, keepdims=True)
        outs.append(_dot(p.astype(BF16), mv[:, sl]))
    return jnp.concatenate(outs, axis=-1)


def _prompt_branch_kernel(u_ref, halo_ref, qm_ref, mk_ref, mv_ref, pw_ref, ps_ref, op_ref, om_ref, *, tm):
    i = pl.program_id(1)
    halo = jnp.where(i > 0, halo_ref[...], 0.0)
    ext = jnp.concatenate([halo, u_ref[...]], axis=0)
    pos = i * tm - POOL_HALO + lax.broadcasted_iota(jnp.int32, (POOL_HALO + tm, 1), 0)
    cnt_fn = lambda w: jnp.clip(pos + 1, 1, w).astype(F32)
    y = _pool_rows(ext, cnt_fn, pw_ref, ps_ref)
    op_ref[...] = y[POOL_HALO:].astype(op_ref.dtype)
    om_ref[...] = _mem_attend(qm_ref[...], mk_ref[...].astype(BF16),
                              mv_ref[...].astype(BF16)).astype(om_ref.dtype)


def _prompt_branch(u, qm, mk, mv, pw, ps, B, S):
    T = u.shape[0]
    n_mem = mk.shape[0] // B
    tm = _row_tile(S, 512)
    nt = S // tm
    hb = tm // POOL_HALO
    row = lambda b, i: (b * nt + i, 0)
    kern = functools.partial(_prompt_branch_kernel, tm=tm)
    return pl.pallas_call(
        kern,
        out_shape=(jax.ShapeDtypeStruct((T, POOL_W), BF16), jax.ShapeDtypeStruct((T, MEM_W), BF16)),
        grid=(B, nt),
        in_specs=[
            pl.BlockSpec((tm, POOL_W), row),
            pl.BlockSpec((POOL_HALO, POOL_W), lambda b, i: (jnp.maximum((b * nt + i) * hb - 1, 0), 0)),
            pl.BlockSpec((tm, MEM_W), row),
            pl.BlockSpec((n_mem, MEM_W), lambda b, i: (b, 0)),
            pl.BlockSpec((n_mem, MEM_W), lambda b, i: (b, 0)),
            pl.BlockSpec((len(POOL_WINDOWS), POOL_GC, POOL_GC), lambda b, i: (0, 0, 0)),
            pl.BlockSpec((1, POOL_W), lambda b, i: (0, 0)),
        ],
        out_specs=(pl.BlockSpec((tm, POOL_W), row), pl.BlockSpec((tm, MEM_W), row)),
        compiler_params=_cparams(("parallel", "arbitrary")),
        name="prompt_branch",
    )(u, u, qm, mk, mv, pw, ps)


def _sample_branch_kernel(ext_ref, qm_ref, mk_ref, mv_ref, pw_ref, ps_ref, op_ref, om_ref, *, nb):
    y = _pool_rows(ext_ref[...], lambda w: float(w), pw_ref, ps_ref)
    op_ref[...] = y.astype(op_ref.dtype)
    shape = (qm_ref.shape[1], mk_ref.shape[1])
    rj = lax.broadcasted_iota(jnp.int32, shape, 0)
    cj = lax.broadcasted_iota(jnp.int32, shape, 1)
    own = (cj & (MEM_HEADS - 1)) == (rj >> 2)
    for n in range(nb):
        s = jnp.where(own, _dot_nt(qm_ref[n], mk_ref[n].astype(BF16)) * MEM_SCALE, NEG_INF)
        e = jnp.exp(s - jnp.max(s, axis=-1, keepdims=True))
        p = e / jnp.sum(e, axis=-1, keepdims=True)
        om_ref[n] = _dot(p.astype(BF16), mv_ref[n].astype(BF16)).astype(om_ref.dtype)


def _sample_branch(ext, qm16, mk, mv, pw, ps, ext_rows):
    Bs, mem_rows, _ = mk.shape
    q_rows = qm16.shape[1]
    nb = _row_tile(Bs, 8)
    kern = functools.partial(_sample_branch_kernel, nb=nb)
    seq3 = lambda i: (i, 0, 0)
    return pl.pallas_call(
        kern,
        out_shape=(jax.ShapeDtypeStruct((Bs * ext_rows, POOL_W), BF16),
                   jax.ShapeDtypeStruct((Bs, q_rows, MEM_HEAD_DIM), BF16)),
        grid=(Bs // nb,),
        in_specs=[
            pl.BlockSpec((nb * ext_rows, POOL_W), lambda i: (i, 0)),
            pl.BlockSpec((nb, q_rows, MEM_HEAD_DIM), seq3),
            pl.BlockSpec((nb, mem_rows, MEM_HEAD_DIM), seq3),
            pl.BlockSpec((nb, mem_rows, MEM_HEAD_DIM), seq3),
            pl.BlockSpec((len(POOL_WINDOWS), POOL_GC, POOL_GC), lambda i: (0, 0, 0)),
            pl.BlockSpec((1, POOL_W), lambda i: (0, 0)),
        ],
        out_specs=(pl.BlockSpec((nb * ext_rows, POOL_W), lambda i: (i, 0)),
                   pl.BlockSpec((nb, q_rows, MEM_HEAD_DIM), seq3)),
        compiler_params=_cparams(("parallel",)),
        name="sample_branch",
    )(ext, qm16, mk, mv, pw, ps)


def _mem_kv_kernel(mem_ref, g_ref, w_ref, k_ref, v_ref):
    h = _rms(mem_ref[...], g_ref[...]).astype(BF16)
    kv = _dot(h, w_ref[...])
    k_ref[...] = kv[:, :MEM_W]
    v_ref[...] = kv[:, MEM_W:]


def _mem_kv(mem, g, w):
    R, D = mem.shape
    tm = _row_tile(R, 256)
    row = lambda i: (i, 0)
    const = lambda i: (0, 0)
    return pl.pallas_call(
        _mem_kv_kernel,
        out_shape=(jax.ShapeDtypeStruct((R, MEM_W), F32), jax.ShapeDtypeStruct((R, MEM_W), F32)),
        grid=(R // tm,),
        in_specs=[pl.BlockSpec((tm, D), row), pl.BlockSpec((1, D), const),
                  pl.BlockSpec((D, 2 * MEM_W), const)],
        out_specs=(pl.BlockSpec((tm, MEM_W), row), pl.BlockSpec((tm, MEM_W), row)),
        compiler_params=_cparams(("parallel",)),
        name="mem_kv",
    )(mem, g, w)


def _merge_kernel(x_ref, g1_ref, wga_ref, wgp_ref, wgm_ref, oa_ref, op_ref, om_ref,
                  wa_ref, wp_ref, wm_ref, wo_ref, x1_ref):
    x = x_ref[...]
    h = _rms(x, g1_ref[...]).astype(BF16)

    def gate(wg_ref):
        return jax.nn.sigmoid(_dot(h, wg_ref[...]))

    mix = gate(wga_ref) * _dot(oa_ref[...], wa_ref[...])
    mix = mix + gate(wgp_ref) * _dot(op_ref[...], wp_ref[...])
    mix = mix + gate(wgm_ref) * _dot(om_ref[...], wm_ref[...])
    x1_ref[...] = x + _dot(mix.astype(BF16), wo_ref[...])


def _merge(x, g1, w_in, oa, op, om, wa, wp, wm, wo):
    T, D = x.shape
    tm = _row_tile(T, 512)
    gate0 = w_in.shape[1] // D - 3
    row = lambda i: (i, 0)
    const = lambda i: (0, 0)
    full = lambda a: pl.BlockSpec(a.shape, const)
    return pl.pallas_call(
        _merge_kernel,
        out_shape=jax.ShapeDtypeStruct((T, D), F32),
        grid=(T // tm,),
        in_specs=[pl.BlockSpec((tm, D), row), full(g1),
                  pl.BlockSpec((D, D), lambda i: (0, gate0)),
                  pl.BlockSpec((D, D), lambda i: (0, gate0 + 1)),
                  pl.BlockSpec((D, D), lambda i: (0, gate0 + 2)),
                  pl.BlockSpec((tm, ATTN_W), row), pl.BlockSpec((tm, POOL_W), row),
                  pl.BlockSpec((tm, MEM_W), row), full(wa), full(wp), full(wm), full(wo)],
        out_specs=pl.BlockSpec((tm, D), row),
        compiler_params=_cparams(("parallel",)),
        name="merge",
    )(x, g1, w_in, w_in, w_in, oa, op, om, wa, wp, wm, wo)


def _ffn_kernel(x1_ref, g2_ref, wg_ref, wu_ref, wd_ref, gf_ref, y_ref):
    x1 = x1_ref[...]
    h2 = _rms(x1, g2_ref[...]).astype(BF16)
    a = _dot(h2, wg_ref[...])
    b = _dot(h2, wu_ref[...])
    t = (jax.nn.silu(a) * b).astype(BF16)
    x2 = x1 + _dot(t, wd_ref[...])
    y_ref[...] = _rms(x2, gf_ref[...])


def _ffn(x1, g2, wg, wu, wd, gf):
    T, D = x1.shape
    tm = _row_tile(T, 256)
    row = lambda i: (i, 0)
    const = lambda i: (0, 0)
    full = lambda a: pl.BlockSpec(a.shape, const)
    return pl.pallas_call(
        _ffn_kernel,
        out_shape=jax.ShapeDtypeStruct((T, D), F32),
        grid=(T // tm,),
        in_specs=[pl.BlockSpec((tm, D), row), full(g2), full(wg), full(wu), full(wd), full(gf)],
        out_specs=pl.BlockSpec((tm, D), row),
        compiler_params=_cparams(("parallel",)),
        name="ffn",
    )(x1, g2, wg, wu, wd, gf)


def _rope_tables(pos):
    half = HEAD_DIM // 2
    inv = ROPE_THETA ** (-jnp.arange(half, dtype=F32) / half)
    ang = pos.astype(F32)[:, None] * inv[None, :]
    cos = jnp.cos(ang)
    sin = jnp.sin(ang)
    return jnp.tile(cos, (1, 4)), jnp.concatenate([-sin, sin, -sin, sin], axis=1)


def kernel(x_prompt, x_sample, cache_k, cache_v, page_table, state_pool, cache_mem_k, cache_mem_v,
           mem_prompt, norm1_g, w_in, lambda_q1, lambda_k1, lambda_q2, lambda_k2, subln_g,
           pool_w, pool_scale, norm_mem_g, w_mem_kv, w_attn_proj, w_pool_proj, w_mem_proj, w_out,
           norm2_g, w_ffn_gate, w_ffn_up, w_ffn_down, norm_f_g):
    B, S, D = x_prompt.shape
    Bs, Ss, _ = x_sample.shape
    assert Ss == 4 and D == ATTN_W
    depth = w_in.shape[0]
    assert depth == 1
    n_pool, page = cache_k.shape[1], cache_k.shape[2]
    past_len = page_table.shape[1] * page
    n_mem = mem_prompt.shape[1]
    row2 = lambda a: a.reshape(1, -1)

    cos_p, sin_p = _rope_tables(jnp.arange(S, dtype=jnp.int32))
    cos_s, sin_s = _rope_tables(past_len + jnp.arange(Ss, dtype=jnp.int32))
    cos_s = jnp.tile(cos_s, (Bs, 1))
    sin_s = jnp.tile(sin_s, (Bs, 1))

    l = 0
    lam_init = 0.8 - 0.6 * math.exp(-0.3 * l)
    w_in_b = w_in[l].astype(BF16)
    g1 = row2(norm1_g[l])
    lam_args = (row2(lambda_q1[l]), row2(lambda_k1[l]), row2(lambda_q2[l]), row2(lambda_k2[l]),
                row2(subln_g[l]))
    pw = pool_w[l].astype(BF16)
    ps = row2(pool_scale[l])
    dense_w = (w_attn_proj[l].astype(BF16), w_pool_proj[l].astype(BF16), w_mem_proj[l].astype(BF16),
               w_out[l].astype(BF16))
    ffn_w = (row2(norm2_g[l]), w_ffn_gate[l].astype(BF16), w_ffn_up[l].astype(BF16),
             w_ffn_down[l].astype(BF16), row2(norm_f_g))

    xp = x_prompt.reshape(B * S, D)
    xs = x_sample.reshape(Bs * Ss, D)
    mk_p, mv_p = _mem_kv(mem_prompt.reshape(B * n_mem, D), row2(norm_mem_g[l]), w_mem_kv[l].astype(BF16))
    q_p, kf_p, kb_p, vf_p, vt_p, u_p, qm_p = _in_proj(xp, g1, w_in_b, cos_p, sin_p)
    q_s, kf_s, kb_s, vf_s, _, u_s, qm_s = _in_proj(xs, g1, w_in_b, cos_s, sin_s)

    q_s3 = q_s.reshape(Bs, Ss, ATTN_W).astype(F32)
    q8 = jnp.concatenate([q_s3, q_s3], axis=1)
    head_rows = lambda a, n: a.reshape(n, -1, V_DIM)
    score_rows = N_HEADS * 8
    oa_s8, oa_p = _fused_attn(page_table, q8, head_rows(kb_s, Bs), head_rows(vf_s.astype(BF16), Bs),
                              _page_bias(score_rows, page * N_HEADS // HEAD_GRP),
                              _new_bias(score_rows, Ss * N_HEADS), *lam_args,
                              head_rows(cache_k[l], n_pool), head_rows(cache_v[l], n_pool),
                              q_p, kb_p, vt_p, B, S, lam_init)
    oa_s = oa_s8[:, :Ss].reshape(Bs * Ss, ATTN_W).astype(BF16)

    op_p, om_p = _prompt_branch(u_p, qm_p, mk_p, mv_p, pw, ps, B, S)
    x1_p = _merge(xp, g1, w_in_b, oa_p, op_p, om_p, *dense_w)
    y_p = _ffn(x1_p, *ffn_w)

    ext_rows = 1 + POOL_BUF + Ss + 4
    u_s3 = u_s.reshape(Bs, Ss, POOL_W)
    ext = jnp.concatenate([jnp.zeros((Bs, 1, POOL_W), F32), state_pool[l], u_s3,
                           jnp.zeros((Bs, 4, POOL_W), F32)], axis=1).reshape(Bs * ext_rows, POOL_W)
    qm_hq = qm_s.reshape(Bs, Ss, MEM_HEADS, MEM_HEAD_DIM).transpose(0, 2, 1, 3)
    op_s_full, om_hq = _sample_branch(ext, qm_hq.reshape(Bs, MEM_HEADS * Ss, MEM_HEAD_DIM),
                                      cache_mem_k[l].reshape(Bs, n_mem * MEM_HEADS, MEM_HEAD_DIM),
                                      cache_mem_v[l].reshape(Bs, n_mem * MEM_HEADS, MEM_HEAD_DIM),
                                      pw, ps, ext_rows)
    op_s = op_s_full.reshape(Bs, ext_rows, POOL_W)[:, 1 + POOL_BUF:1 + POOL_BUF + Ss].reshape(Bs * Ss, POOL_W)
    om_s = om_hq.reshape(Bs, MEM_HEADS, Ss, MEM_HEAD_DIM).transpose(0, 2, 1, 3).reshape(Bs * Ss, MEM_W)
    x1_s = _merge(xs, g1, w_in_b, oa_s, op_s, om_s, *dense_w)
    y_s = _ffn(x1_s, *ffn_w)

    pool_p = u_p.reshape(B, S, POOL_W)[:, S - POOL_BUF:]
    pool_s = jnp.concatenate([state_pool[l][:, Ss:], u_s3], axis=1)
    hshape = (N_HEADS, V_DIM)
    mshape = (MEM_HEADS, MEM_HEAD_DIM)
    return (y_p.reshape(B, S, D), y_s.reshape(Bs, Ss, D),
            kf_p.reshape(1, B, S, *hshape), vf_p.reshape(1, B, S, *hshape),
            pool_p[None],
            mk_p.reshape(1, B, n_mem, *mshape), mv_p.reshape(1, B, n_mem, *mshape),
            kf_s.reshape(1, Bs, Ss, *hshape), vf_s.reshape(1, Bs, Ss, *hshape),
            pool_s[None])
```

```python
import functools
import math

import jax
import jax.numpy as jnp
import numpy as np
from jax import lax
from jax.experimental import pallas as pl
from jax.experimental.pallas import tpu as pltpu

F32 = jnp.float32
BF16 = jnp.bfloat16

N_HEADS = 8
HEAD_DIM = 64
V_DIM = 2 * HEAD_DIM
ATTN_W = N_HEADS * V_DIM
ROPE_THETA = 10000.0
POOL_WINDOWS = (2, 4, 8, 16)
POOL_GC = 128
POOL_W = len(POOL_WINDOWS) * POOL_GC
POOL_BUF = max(POOL_WINDOWS) - 1
POOL_HALO = 16
MEM_HEADS = 4
MEM_HEAD_DIM = 128
MEM_W = MEM_HEADS * MEM_HEAD_DIM
EPS = 1e-5
NEG_INF = -1e30
Q_SCALE = HEAD_DIM ** -0.5 * math.log2(math.e)
MEM_SCALE = MEM_HEAD_DIM ** -0.5

VMEM_LIMIT = 56 * 1024 * 1024


def _cparams(sem):
    return pltpu.CompilerParams(dimension_semantics=sem, vmem_limit_bytes=VMEM_LIMIT)


def _rms(x, g):
    return x * lax.rsqrt(jnp.mean(x * x, axis=-1, keepdims=True) + EPS) * g


def _dot(a, b):
    return jnp.dot(a, b, preferred_element_type=F32)


def _dot_nt(a, b):
    return lax.dot_general(a, b, (((1,), (1,)), ((), ())), preferred_element_type=F32)


def _row_tile(n, pref):
    t = min(pref, n)
    while n % t:
        t //= 2
    return t


def _in_proj_kernel(x_ref, g_ref, wq_ref, wk_ref, wv_ref, wum_ref, cos_ref, sin_ref,
                    q_ref, kf_ref, kb_ref, vf_ref, vt_ref, u_ref, qm_ref):
    h = _rms(x_ref[...], g_ref[...]).astype(BF16)
    cos = cos_ref[...]
    sin = sin_ref[...]
    lane = lax.broadcasted_iota(jnp.int32, cos.shape, 1)
    first_half = (lane & (HEAD_DIM - 1)) < HEAD_DIM // 2

    def rotary(zh):
        partner = jnp.where(first_half,
                            pltpu.roll(zh, V_DIM - HEAD_DIM // 2, axis=1),
                            pltpu.roll(zh, HEAD_DIM // 2, axis=1))
        return zh * cos + partner * sin

    zq = _dot(h, wq_ref[...])
    for hd in range(N_HEADS):
        sl = slice(hd * V_DIM, (hd + 1) * V_DIM)
        q_ref[:, sl] = (rotary(zq[:, sl]) * Q_SCALE).astype(BF16)
    zk = _dot(h, wk_ref[...])
    for hd in range(N_HEADS):
        sl = slice(hd * V_DIM, (hd + 1) * V_DIM)
        kr = rotary(zk[:, sl])
        kf_ref[:, sl] = kr
        kb_ref[:, sl] = kr.astype(BF16)
    zv = _dot(h, wv_ref[...])
    vf_ref[...] = zv
    for hd in range(N_HEADS):
        sl = slice(hd * V_DIM, (hd + 1) * V_DIM)
        vt_ref[sl, :] = zv[:, sl].T.astype(BF16)
    zu = _dot(h, wum_ref[...])
    u_ref[...] = zu[:, :POOL_W]
    qm_ref[...] = zu[:, POOL_W:].astype(BF16)


def _in_proj(x, g1, w, cos_t, sin_t):
    T, D = x.shape
    P = cos_t.shape[0]
    tm = _row_tile(math.gcd(T, P), 512)
    npos = P // tm
    assert POOL_W + MEM_W == ATTN_W
    row = lambda i: (i, 0)
    const = lambda i: (0, 0)
    posmap = lambda i: (i % npos, 0)
    outs = (
        jax.ShapeDtypeStruct((T, ATTN_W), BF16),
        jax.ShapeDtypeStruct((T, ATTN_W), F32),
        jax.ShapeDtypeStruct((T, ATTN_W), BF16),
        jax.ShapeDtypeStruct((T, ATTN_W), F32),
        jax.ShapeDtypeStruct((ATTN_W, T), BF16),
        jax.ShapeDtypeStruct((T, POOL_W), F32),
        jax.ShapeDtypeStruct((T, MEM_W), BF16),
    )
    return pl.pallas_call(
        _in_proj_kernel,
        out_shape=outs,
        grid=(T // tm,),
        in_specs=[
            pl.BlockSpec((tm, D), row),
            pl.BlockSpec((1, D), const),
            pl.BlockSpec((D, ATTN_W), lambda i: (0, 0)),
            pl.BlockSpec((D, ATTN_W), lambda i: (0, 1)),
            pl.BlockSpec((D, ATTN_W), lambda i: (0, 2)),
            pl.BlockSpec((D, ATTN_W), lambda i: (0, 3)),
            pl.BlockSpec((tm, V_DIM), posmap),
            pl.BlockSpec((tm, V_DIM), posmap),
        ],
        out_specs=(
            pl.BlockSpec((tm, ATTN_W), row),
            pl.BlockSpec((tm, ATTN_W), row),
            pl.BlockSpec((tm, ATTN_W), row),
            pl.BlockSpec((tm, ATTN_W), row),
            pl.BlockSpec((ATTN_W, tm), lambda i: (0, i)),
            pl.BlockSpec((tm, POOL_W), row),
            pl.BlockSpec((tm, MEM_W), row),
        ),
        compiler_params=_cparams(("parallel",)),
        name="in_proj",
    )(x, g1, w, w, w, w, cos_t, sin_t)


def _diff_lambda(lq1_ref, lk1_ref, lq2_ref, lk2_ref, lam_init):
    a = jnp.sum(lq1_ref[...] * lk1_ref[...], axis=-1, keepdims=True)
    b = jnp.sum(lq2_ref[...] * lk2_ref[...], axis=-1, keepdims=True)
    return jnp.exp(a) - jnp.exp(b) + lam_init


def _sub_norm(o, g, lam_init):
    return _rms(o, g) * (1.0 - lam_init)


ONES_ROWS = 16
N_GRP = 16
N_SUB = 4
HEAD_GRP = 4
P_SLOTS = 3
(S_QBLK, S_HEAD, S_NVALID, S_FIRST, S_LAST, S_KBLK) = range(6)


def _prompt_schedule(B, nq, n_steps):
    rows = []
    for b in range(B):
        for h in range(N_HEADS):
            for qi in range(nq):
                blocks = [qi] + list(range(qi))
                n_chunks = -(-len(blocks) // P_SLOTS)
                bounds = [len(blocks) * c // n_chunks for c in range(n_chunks + 1)]
                for c in range(n_chunks):
                    chunk = blocks[bounds[c]:bounds[c + 1]]
                    kb = [b * nq + j for j in chunk]
                    kb += [kb[-1]] * (P_SLOTS - len(chunk))
                    rows.append([b * nq + qi, h, len(chunk), int(c == 0), int(c == n_chunks - 1)] + kb)
    assert len(rows) <= n_steps, (len(rows), n_steps)
    idle = rows[-1][:2] + [0, 0, 0] + rows[-1][S_KBLK:]
    rows += [idle] * (n_steps - len(rows))
    return np.asarray(rows, np.int32).T


def _fused_attn_kernel(pt_ref, sched_ref, q_ref, kn_ref, vn_ref, bias_ref, biasn_ref,
                       lq1_ref, lk1_ref, lq2_ref, lk2_ref, sg_ref, pq_ref, *rest, n_grp, tq, lam_init):
    del pt_ref
    pk_refs = rest[:P_SLOTS]
    pvt_refs = rest[P_SLOTS:2 * P_SLOTS]
    rest = rest[2 * P_SLOTS:]
    k_refs = rest[:n_grp]
    v_refs = rest[n_grp:2 * n_grp]
    o_ref, po_ref = rest[2 * n_grp:2 * n_grp + 2]
    qw_ref, qrows_ref, m_ref, l_ref, acc_ref, pm_ref, pacc_ref = rest[2 * n_grp + 2:2 * n_grp + 9]
    st0_ref, qq_ref = rest[2 * n_grp + 9:]
    g = pl.program_id(1)
    step = pl.program_id(0) * pl.num_programs(1) + g


    @pl.when(g == 0)
    def _():
        q8 = q_ref[0]
        r8 = lax.broadcasted_iota(jnp.int32, (8, V_DIM), 0)
        c8 = lax.broadcasted_iota(jnp.int32, (8, V_DIM), 1)
        own_map = (c8 >= HEAD_DIM) == (r8 >= 4)
        blocks = [jnp.where(own_map, q8[:, hd * V_DIM:(hd + 1) * V_DIM], 0.0) for hd in range(N_HEADS)]
        qrows_ref[...] = jnp.concatenate(blocks, axis=0).astype(BF16)
        zero = jnp.zeros_like(blocks[0])
        qw_ref[...] = jnp.concatenate(
            [jnp.concatenate([blocks[hd] if hd % HEAD_GRP == a else zero for a in range(HEAD_GRP)], axis=1)
             for hd in range(N_HEADS)], axis=0).astype(BF16)
        m_ref[...] = jnp.full(m_ref.shape, NEG_INF, F32)
        l_ref[...] = jnp.zeros(l_ref.shape, F32)
        acc_ref[...] = jnp.zeros(acc_ref.shape, F32)

    def view(ref):
        n = ref.shape[1] // HEAD_GRP
        return jnp.concatenate([ref[0, pl.ds(a, n, stride=HEAD_GRP), :].astype(BF16)
                                for a in range(HEAD_GRP)], axis=1)

    def update(m, l, acc, s, pv_fn):
        m_new = jnp.maximum(m, jnp.max(s, axis=-1, keepdims=True))
        alpha = jnp.exp2(m - m_new)
        p = jnp.exp2(s - m_new)
        l = alpha * l + jnp.sum(p, axis=-1, keepdims=True)
        acc = alpha * acc + pv_fn(p.astype(BF16))
        return m_new, l, acc

    qw = qw_ref[...]
    bias = bias_ref[...]
    pq = pq_ref[...]
    lane = lax.broadcasted_iota(jnp.int32, pq.shape, 1)
    pzero = jnp.zeros_like(pq)
    qq = jnp.concatenate([jnp.where(lane < HEAD_DIM, pq, pzero),
                          jnp.where(lane >= HEAD_DIM, pq, pzero)], axis=0)
    n = bias.shape[1]
    n_sub = min(N_SUB, n_grp)
    sub = n_grp // n_sub
    qq_ref[...] = qq
    scores = []
    for u in range(n_sub):
        scores.append(jnp.concatenate(
            [_dot_nt(qw, view(k_refs[u * sub + i])) + bias for i in range(sub)], axis=1))
        if u == 0:
            st0_ref[...] = _dot_nt(pk_refs[0][...], qq)
    m, l, acc = m_ref[...], l_ref[...], acc_ref[...]
    for u in range(n_sub):
        def past_pv(p, u=u):
            pv = _dot(p[:, 0:n], view(v_refs[u * sub]))
            for i in range(1, sub):
                pv = pv + _dot(p[:, i * n:(i + 1) * n], view(v_refs[u * sub + i]))
            return pv
        m, l, acc = update(m, l, acc, scores[u], past_pv)
    m_ref[...], l_ref[...], acc_ref[...] = m, l, acc

    @pl.when(g == pl.num_programs(1) - 1)
    def _():
        sn = _dot_nt(qrows_ref[...], kn_ref[0]) + biasn_ref[...]
        _, l2, acc2 = update(m, l, acc, sn,
                             lambda p: jnp.concatenate([_dot(p, vn_ref[0])] * HEAD_GRP, axis=1))
        lam = _diff_lambda(lq1_ref, lk1_ref, lq2_ref, lk2_ref, lam_init)
        on = acc2 / l2
        for hd in range(N_HEADS):
            a = hd % HEAD_GRP
            blk = on[hd * 8:(hd + 1) * 8, a * V_DIM:(a + 1) * V_DIM]
            o8 = blk - lam * pltpu.roll(blk, 4, axis=0)
            o_ref[0, :, hd * V_DIM:(hd + 1) * V_DIM] = _sub_norm(o8, sg_ref[...], lam_init)

    nvalid = sched_ref[S_NVALID, step]
    first = sched_ref[S_FIRST, step]
    last = sched_ref[S_LAST, step]

    def consume(st, slot, masked):
        if masked:
            r = lax.broadcasted_iota(jnp.int32, st.shape, 0)
            c = lax.broadcasted_iota(jnp.int32, st.shape, 1)
            c = jnp.where(c >= tq, c - tq, c)
            st = jnp.where(r <= c, st, NEG_INF)
        pm = pm_ref[...]
        pm_new = jnp.maximum(pm, jnp.max(st, axis=0, keepdims=True))
        alpha = jnp.exp2(pm - pm_new)
        pt = jnp.exp2(st - pm_new).astype(BF16)
        vt1 = jnp.concatenate([pvt_refs[slot][...], jnp.ones((ONES_ROWS, tq), BF16)], axis=0)
        pacc_ref[...] = alpha * pacc_ref[...] + _dot(vt1, pt)
        pm_ref[...] = pm_new

    def slots_block(nv, is_first):
        if is_first:
            pm_ref[...] = jnp.full(pm_ref.shape, NEG_INF, F32)
            pacc_ref[...] = jnp.zeros(pacc_ref.shape, F32)
        st = st0_ref[...]
        for slot in range(nv):
            nxt = _dot_nt(pk_refs[slot + 1][...], qq_ref[...]) if slot + 1 < nv else None
            consume(st, slot, is_first and slot == 0)
            st = nxt

    for nv in range(1, P_SLOTS + 1):
        for is_first in (True, False):
            pl.when((nvalid == nv) & (first == int(is_first)))(functools.partial(slots_block, nv, is_first))

    @pl.when(last == 1)
    def _():
        pacc = pacc_ref[...]
        on = pacc[:V_DIM] / pacc[V_DIM:V_DIM + 1]
        lam = _diff_lambda(lq1_ref, lk1_ref, lq2_ref, lk2_ref, lam_init)
        o = (on[:, :tq] - lam * on[:, tq:]).T
        po_ref[...] = _sub_norm(o, sg_ref[...], lam_init).astype(po_ref.dtype)


def _fused_attn(page_table, q8, kn, vn, bias, biasn, lq1, lk1, lq2, lk2, sg, ck, cv,
                pq, pk, pvt, B, S, lam_init):
    Bs, n_pages = page_table.shape
    page_rows = ck.shape[1]
    new_rows = kn.shape[1]
    n_grp = N_GRP
    while n_pages % n_grp:
        n_grp //= 2
    n_groups = n_pages // n_grp
    T = pq.shape[0]
    tq = _row_tile(S, 512)
    sched = jnp.asarray(_prompt_schedule(B, S // tq, Bs * n_groups))
    step = lambda b, g: b * n_groups + g
    vec = lambda b, g, pt, sc: (0, 0)
    seq = lambda b, g, pt, sc: (b, 0, 0)
    qmap = lambda b, g, pt, sc: (sc[S_QBLK, step(b, g)], sc[S_HEAD, step(b, g)])

    def page_spec(i):
        return pl.BlockSpec((1, page_rows, V_DIM), lambda b, g, pt, sc: (pt[b, g * n_grp + i], 0, 0))

    def pk_spec(i):
        return pl.BlockSpec((tq, V_DIM), lambda b, g, pt, sc: (sc[S_KBLK + i, step(b, g)], sc[S_HEAD, step(b, g)]))

    def pvt_spec(i):
        return pl.BlockSpec((V_DIM, tq), lambda b, g, pt, sc: (sc[S_HEAD, step(b, g)], sc[S_KBLK + i, step(b, g)]))

    kern = functools.partial(_fused_attn_kernel, n_grp=n_grp, tq=tq, lam_init=lam_init)
    grid_spec = pltpu.PrefetchScalarGridSpec(
        num_scalar_prefetch=2,
        grid=(Bs, n_groups),
        in_specs=[
            pl.BlockSpec((1, 8, ATTN_W), seq),
            pl.BlockSpec((1, new_rows, V_DIM), seq),
            pl.BlockSpec((1, new_rows, V_DIM), seq),
            pl.BlockSpec(bias.shape, vec),
            pl.BlockSpec(biasn.shape, vec),
            pl.BlockSpec((1, HEAD_DIM), vec),
            pl.BlockSpec((1, HEAD_DIM), vec),
            pl.BlockSpec((1, HEAD_DIM), vec),
            pl.BlockSpec((1, HEAD_DIM), vec),
            pl.BlockSpec((1, V_DIM), vec),
            pl.BlockSpec((tq, V_DIM), qmap),
        ] + [pk_spec(i) for i in range(P_SLOTS)] + [pvt_spec(i) for i in range(P_SLOTS)]
          + [page_spec(i) for i in range(n_grp)] + [page_spec(i) for i in range(n_grp)],
        out_specs=(pl.BlockSpec((1, 8, ATTN_W), seq), pl.BlockSpec((tq, V_DIM), qmap)),
        scratch_shapes=[
            pltpu.VMEM((N_HEADS * 8, HEAD_GRP * V_DIM), BF16),
            pltpu.VMEM((N_HEADS * 8, V_DIM), BF16),
            pltpu.VMEM((N_HEADS * 8, 1), F32),
            pltpu.VMEM((N_HEADS * 8, 1), F32),
            pltpu.VMEM((N_HEADS * 8, HEAD_GRP * V_DIM), F32),
            pltpu.VMEM((1, 2 * tq), F32),
            pltpu.VMEM((V_DIM + ONES_ROWS, 2 * tq), F32),
            pltpu.VMEM((tq, 2 * tq), F32),
            pltpu.VMEM((2 * tq, V_DIM), BF16),
        ],
    )
    return pl.pallas_call(
        kern,
        out_shape=(jax.ShapeDtypeStruct((Bs, 8, ATTN_W), F32), jax.ShapeDtypeStruct((T, ATTN_W), BF16)),
        grid_spec=grid_spec,
        compiler_params=_cparams(("arbitrary", "arbitrary")),
        name="fused_attn",
    )(page_table, sched, q8, kn, vn, bias, biasn, lq1, lk1, lq2, lk2, sg, pq,
      *([pk] * P_SLOTS), *([pvt] * P_SLOTS), *([ck] * n_grp), *([cv] * n_grp))


def _page_bias(rows, cols):
    rj = jnp.arange(rows, dtype=jnp.int32)[:, None]
    cj = jnp.arange(cols, dtype=jnp.int32)[None, :]
    n_hg = N_HEADS // HEAD_GRP
    return jnp.where((cj % n_hg) == (rj // 8) // HEAD_GRP, 0.0, NEG_INF).astype(F32)


def _new_bias(rows, cols):
    rj = jnp.arange(rows, dtype=jnp.int32)[:, None]
    cj = jnp.arange(cols, dtype=jnp.int32)[None, :]
    keep = ((cj % N_HEADS) == (rj // 8)) & ((cj // N_HEADS) <= (rj % 4))
    return jnp.where(keep, 0.0, NEG_INF).astype(F32)


def _pool_rows(ext, cnt_fn, pw_ref, ps_ref):
    outs = []
    for gi, w in enumerate(POOL_WINDOWS):
        tok = ext[:, gi * POOL_GC:(gi + 1) * POOL_GC]
        cur = tok
        sh = 1
        while sh < w:
            cur = cur + pltpu.roll(cur, sh, axis=0)
            sh *= 2
        d = cur / cnt_fn(w) - tok
        outs.append(_dot(d.astype(BF16), pw_ref[gi]))
    return jnp.concatenate(outs, axis=-1) * ps_ref[...]


def _mem_attend(qm, mk, mv):
    outs = []
    for hd in range(MEM_HEADS):
        sl = slice(hd * MEM_HEAD_DIM, (hd + 1) * MEM_HEAD_DIM)
        s = _dot_nt(qm[:, sl], mk[:, sl]) * MEM_SCALE
        s = s - jnp.max(s, axis=-1, keepdims=True)
        e = jnp.exp(s)
        p = e / jnp.sum(e, axis=-1, keepdims=True)
        outs.append(_dot(p.astype(BF16), mv[:, sl]))
    return jnp.concatenate(outs, axis=-1)


def _prompt_branch_kernel(u_ref, halo_ref, qm_ref, mk_ref, mv_ref, pw_ref, ps_ref, op_ref, om_ref, *, tm):
    i = pl.program_id(1)
    halo = jnp.where(i > 0, halo_ref[...], 0.0)
    ext = jnp.concatenate([halo, u_ref[...]], axis=0)
    pos = i * tm - POOL_HALO + lax.broadcasted_iota(jnp.int32, (POOL_HALO + tm, 1), 0)
    cnt_fn = lambda w: jnp.clip(pos + 1, 1, w).astype(F32)
    y = _pool_rows(ext, cnt_fn, pw_ref, ps_ref)
    op_ref[...] = y[POOL_HALO:].astype(op_ref.dtype)
    om_ref[...] = _mem_attend(qm_ref[...], mk_ref[...].astype(BF16),
                              mv_ref[...].astype(BF16)).astype(om_ref.dtype)


def _prompt_branch(u, qm, mk, mv, pw, ps, B, S):
    T = u.shape[0]
    n_mem = mk.shape[0] // B
    tm = _row_tile(S, 512)
    nt = S // tm
    hb = tm // POOL_HALO
    row = lambda b, i: (b * nt + i, 0)
    kern = functools.partial(_prompt_branch_kernel, tm=tm)
    return pl.pallas_call(
        kern,
        out_shape=(jax.ShapeDtypeStruct((T, POOL_W), BF16), jax.ShapeDtypeStruct((T, MEM_W), BF16)),
        grid=(B, nt),
        in_specs=[
            pl.BlockSpec((tm, POOL_W), row),
            pl.BlockSpec((POOL_HALO, POOL_W), lambda b, i: (jnp.maximum((b * nt + i) * hb - 1, 0), 0)),
            pl.BlockSpec((tm, MEM_W), row),
            pl.BlockSpec((n_mem, MEM_W), lambda b, i: (b, 0)),
            pl.BlockSpec((n_mem, MEM_W), lambda b, i: (b, 0)),
            pl.BlockSpec((len(POOL_WINDOWS), POOL_GC, POOL_GC), lambda b, i: (0, 0, 0)),
            pl.BlockSpec((1, POOL_W), lambda b, i: (0, 0)),
        ],
        out_specs=(pl.BlockSpec((tm, POOL_W), row), pl.BlockSpec((tm, MEM_W), row)),
        compiler_params=_cparams(("parallel", "arbitrary")),
        name="prompt_branch",
    )(u, u, qm, mk, mv, pw, ps)


def _sample_branch_kernel(ext_ref, qm_ref, mk_ref, mv_ref, pw_ref, ps_ref, op_ref, om_ref, *, nb):
    y = _pool_rows(ext_ref[...], lambda w: float(w), pw_ref, ps_ref)
    op_ref[...] = y.astype(op_ref.dtype)
    shape = (qm_ref.shape[1], mk_ref.shape[1])
    rj = lax.broadcasted_iota(jnp.int32, shape, 0)
    cj = lax.broadcasted_iota(jnp.int32, shape, 1)
    own = (cj & (MEM_HEADS - 1)) == (rj >> 2)
    for n in range(nb):
        s = jnp.where(own, _dot_nt(qm_ref[n], mk_ref[n].astype(BF16)) * MEM_SCALE, NEG_INF)
        e = jnp.exp(s - jnp.max(s, axis=-1, keepdims=True))
        p = e / jnp.sum(e, axis=-1, keepdims=True)
        om_ref[n] = _dot(p.astype(BF16), mv_ref[n].astype(BF16)).astype(om_ref.dtype)


def _sample_branch(ext, qm16, mk, mv, pw, ps, ext_rows):
    Bs, mem_rows, _ = mk.shape
    q_rows = qm16.shape[1]
    nb = _row_tile(Bs, 8)
    kern = functools.partial(_sample_branch_kernel, nb=nb)
    seq3 = lambda i: (i, 0, 0)
    return pl.pallas_call(
        kern,
        out_shape=(jax.ShapeDtypeStruct((Bs * ext_rows, POOL_W), BF16),
                   jax.ShapeDtypeStruct((Bs, q_rows, MEM_HEAD_DIM), BF16)),
        grid=(Bs // nb,),
        in_specs=[
            pl.BlockSpec((nb * ext_rows, POOL_W), lambda i: (i, 0)),
            pl.BlockSpec((nb, q_rows, MEM_HEAD_DIM), seq3),
            pl.BlockSpec((nb, mem_rows, MEM_HEAD_DIM), seq3),
            pl.BlockSpec((nb, mem_rows, MEM_HEAD_DIM), seq3),
            pl.BlockSpec((len(POOL_WINDOWS), POOL_GC, POOL_GC), lambda i: (0, 0, 0)),
            pl.BlockSpec((1, POOL_W), lambda i: (0, 0)),
        ],
        out_specs=(pl.BlockSpec((nb * ext_rows, POOL_W), lambda i: (i, 0)),
                   pl.BlockSpec((nb, q_rows, MEM_HEAD_DIM), seq3)),
        compiler_params=_cparams(("parallel",)),
        name="sample_branch",
    )(ext, qm16, mk, mv, pw, ps)


def _mem_kv_kernel(mem_ref, g_ref, w_ref, k_ref, v_ref):
    h = _rms(mem_ref[...], g_ref[...]).astype(BF16)
    kv = _dot(h, w_ref[...])
    k_ref[...] = kv[:, :MEM_W]
    v_ref[...] = kv[:, MEM_W:]


def _mem_kv(mem, g, w):
    R, D = mem.shape
    tm = _row_tile(R, 256)
    row = lambda i: (i, 0)
    const = lambda i: (0, 0)
    return pl.pallas_call(
        _mem_kv_kernel,
        out_shape=(jax.ShapeDtypeStruct((R, MEM_W), F32), jax.ShapeDtypeStruct((R, MEM_W), F32)),
        grid=(R // tm,),
        in_specs=[pl.BlockSpec((tm, D), row), pl.BlockSpec((1, D), const),
                  pl.BlockSpec((D, 2 * MEM_W), const)],
        out_specs=(pl.BlockSpec((tm, MEM_W), row), pl.BlockSpec((tm, MEM_W), row)),
        compiler_params=_cparams(("parallel",)),
        name="mem_kv",
    )(mem, g, w)


def _merge_kernel(x_ref, g1_ref, wga_ref, wgp_ref, wgm_ref, oa_ref, op_ref, om_ref,
                  wa_ref, wp_ref, wm_ref, wo_ref, x1_ref):
    x = x_ref[...]
    h = _rms(x, g1_ref[...]).astype(BF16)

    def gate(wg_ref):
        return jax.nn.sigmoid(_dot(h, wg_ref[...]))

    mix = gate(wga_ref) * _dot(oa_ref[...], wa_ref[...])
    mix = mix + gate(wgp_ref) * _dot(op_ref[...], wp_ref[...])
    mix = mix + gate(wgm_ref) * _dot(om_ref[...], wm_ref[...])
    x1_ref[...] = x + _dot(mix.astype(BF16), wo_ref[...])


def _merge(x, g1, w_in, oa, op, om, wa, wp, wm, wo):
    T, D = x.shape
    tm = _row_tile(T, 512)
    gate0 = w_in.shape[1] // D - 3
    row = lambda i: (i, 0)
    const = lambda i: (0, 0)
    full = lambda a: pl.BlockSpec(a.shape, const)
    return pl.pallas_call(
        _merge_kernel,
        out_shape=jax.ShapeDtypeStruct((T, D), F32),
        grid=(T // tm,),
        in_specs=[pl.BlockSpec((tm, D), row), full(g1),
                  pl.BlockSpec((D, D), lambda i: (0, gate0)),
                  pl.BlockSpec((D, D), lambda i: (0, gate0 + 1)),
                  pl.BlockSpec((D, D), lambda i: (0, gate0 + 2)),
                  pl.BlockSpec((tm, ATTN_W), row), pl.BlockSpec((tm, POOL_W), row),
                  pl.BlockSpec((tm, MEM_W), row), full(wa), full(wp), full(wm), full(wo)],
        out_specs=pl.BlockSpec((tm, D), row),
        compiler_params=_cparams(("parallel",)),
        name="merge",
    )(x, g1, w_in, w_in, w_in, oa, op, om, wa, wp, wm, wo)


def _ffn_kernel(x1_ref, g2_ref, wg_ref, wu_ref, wd_ref, gf_ref, y_ref):
    x1 = x1_ref[...]
    h2 = _rms(x1, g2_ref[...]).astype(BF16)
    a = _dot(h2, wg_ref[...])
    b = _dot(h2, wu_ref[...])
    t = (jax.nn.silu(a) * b).astype(BF16)
    x2 = x1 + _dot(t, wd_ref[...])
    y_ref[...] = _rms(x2, gf_ref[...])


def _ffn(x1, g2, wg, wu, wd, gf):
    T, D = x1.shape
    tm = _row_tile(T, 256)
    row = lambda i: (i, 0)
    const = lambda i: (0, 0)
    full = lambda a: pl.BlockSpec(a.shape, const)
    return pl.pallas_call(
        _ffn_kernel,
        out_shape=jax.ShapeDtypeStruct((T, D), F32),
        grid=(T // tm,),
        in_specs=[pl.BlockSpec((tm, D), row), full(g2), full(wg), full(wu), full(wd), full(gf)],
        out_specs=pl.BlockSpec((tm, D), row),
        compiler_params=_cparams(("parallel",)),
        name="ffn",
    )(x1, g2, wg, wu, wd, gf)


def _rope_tables(pos):
    half = HEAD_DIM // 2
    inv = ROPE_THETA ** (-jnp.arange(half, dtype=F32) / half)
    ang = pos.astype(F32)[:, None] * inv[None, :]
    cos = jnp.cos(ang)
    sin = jnp.sin(ang)
    return jnp.tile(cos, (1, 4)), jnp.concatenate([-sin, sin, -sin, sin], axis=1)


def kernel(x_prompt, x_sample, cache_k, cache_v, page_table, state_pool, cache_mem_k, cache_mem_v,
           mem_prompt, norm1_g, w_in, lambda_q1, lambda_k1, lambda_q2, lambda_k2, subln_g,
           pool_w, pool_scale, norm_mem_g, w_mem_kv, w_attn_proj, w_pool_proj, w_mem_proj, w_out,
           norm2_g, w_ffn_gate, w_ffn_up, w_ffn_down, norm_f_g):
    B, S, D = x_prompt.shape
    Bs, Ss, _ = x_sample.shape
    assert Ss == 4 and D == ATTN_W
    depth = w_in.shape[0]
    assert depth == 1
    n_pool, page = cache_k.shape[1], cache_k.shape[2]
    past_len = page_table.shape[1] * page
    n_mem = mem_prompt.shape[1]
    row2 = lambda a: a.reshape(1, -1)

    cos_p, sin_p = _rope_tables(jnp.arange(S, dtype=jnp.int32))
    cos_s, sin_s = _rope_tables(past_len + jnp.arange(Ss, dtype=jnp.int32))
    cos_s = jnp.tile(cos_s, (Bs, 1))
    sin_s = jnp.tile(sin_s, (Bs, 1))

    l = 0
    lam_init = 0.8 - 0.6 * math.exp(-0.3 * l)
    w_in_b = w_in[l].astype(BF16)
    g1 = row2(norm1_g[l])
    lam_args = (row2(lambda_q1[l]), row2(lambda_k1[l]), row2(lambda_q2[l]), row2(lambda_k2[l]),
                row2(subln_g[l]))
    pw = pool_w[l].astype(BF16)
    ps = row2(pool_scale[l])
    dense_w = (w_attn_proj[l].astype(BF16), w_pool_proj[l].astype(BF16), w_mem_proj[l].astype(BF16),
               w_out[l].astype(BF16))
    ffn_w = (row2(norm2_g[l]), w_ffn_gate[l].astype(BF16), w_ffn_up[l].astype(BF16),
             w_ffn_down[l].astype(BF16), row2(norm_f_g))

    xp = x_prompt.reshape(B * S, D)
    xs = x_sample.reshape(Bs * Ss, D)
    mk_p, mv_p = _mem_kv(mem_prompt.reshape(B * n_mem, D), row2(norm_mem_g[l]), w_mem_kv[l].astype(BF16))
    q_p, kf_p, kb_p, vf_p, vt_p, u_p, qm_p = _in_proj(xp, g1, w_in_b, cos_p, sin_p)
    q_s, kf_s, kb_s, vf_s, _, u_s, qm_s = _in_proj(xs, g1, w_in_b, cos_s, sin_s)

    q_s3 = q_s.reshape(Bs, Ss, ATTN_W).astype(F32)
    q8 = jnp.concatenate([q_s3, q_s3], axis=1)
    head_rows = lambda a, n: a.reshape(n, -1, V_DIM)
    score_rows = N_HEADS * 8
    oa_s8, oa_p = _fused_attn(page_table, q8, head_rows(kb_s, Bs), head_rows(vf_s.astype(BF16), Bs),
                              _page_bias(score_rows, page * N_HEADS // HEAD_GRP),
                              _new_bias(score_rows, Ss * N_HEADS), *lam_args,
                              head_rows(cache_k[l], n_pool), head_rows(cache_v[l], n_pool),
                              q_p, kb_p, vt_p, B, S, lam_init)
    oa_s = oa_s8[:, :Ss].reshape(Bs * Ss, ATTN_W).astype(BF16)

    op_p, om_p = _prompt_branch(u_p, qm_p, mk_p, mv_p, pw, ps, B, S)
    x1_p = _merge(xp, g1, w_in_b, oa_p, op_p, om_p, *dense_w)
    y_p = _ffn(x1_p, *ffn_w)

    ext_rows = 1 + POOL_BUF + Ss + 4
    u_s3 = u_s.reshape(Bs, Ss, POOL_W)
    ext = jnp.concatenate([jnp.zeros((Bs, 1, POOL_W), F32), state_pool[l], u_s3,
                           jnp.zeros((Bs, 4, POOL_W), F32)], axis=1).reshape(Bs * ext_rows, POOL_W)
    qm_hq = qm_s.reshape(Bs, Ss, MEM_HEADS, MEM_HEAD_DIM).transpose(0, 2, 1, 3)
    op_s_full, om_hq = _sample_branch(ext, qm_hq.reshape(Bs, MEM_HEADS * Ss, MEM_HEAD_DIM),
                                      cache_mem_k[l].reshape(Bs, n_mem * MEM_HEADS, MEM_HEAD_DIM),
                                      cache_mem_v[l].reshape(Bs, n_mem * MEM_HEADS, MEM_HEAD_DIM),
                                      pw, ps, ext_rows)
    op_s = op_s_full.reshape(Bs, ext_rows, POOL_W)[:, 1 + POOL_BUF:1 + POOL_BUF + Ss].reshape(Bs * Ss, POOL_W)
    om_s = om_hq.reshape(Bs, MEM_HEADS, Ss, MEM_HEAD_DIM).transpose(0, 2, 1, 3).reshape(Bs * Ss, MEM_W)
    x1_s = _merge(xs, g1, w_in_b, oa_s, op_s, om_s, *dense_w)
    y_s = _ffn(x1_s, *ffn_w)

    pool_p = u_p.reshape(B, S, POOL_W)[:, S - POOL_BUF:]
    pool_s = jnp.concatenate([state_pool[l][:, Ss:], u_s3], axis=1)
    hshape = (N_HEADS, V_DIM)
    mshape = (MEM_HEADS, MEM_HEAD_DIM)
    return (y_p.reshape(B, S, D), y_s.reshape(Bs, Ss, D),
            kf_p.reshape(1, B, S, *hshape), vf_p.reshape(1, B, S, *hshape),
            pool_p[None],
            mk_p.reshape(1, B, n_mem, *mshape), mv_p.reshape(1, B, n_mem, *mshape),
            kf_s.reshape(1, Bs, Ss, *hshape), vf_s.reshape(1, Bs, Ss, *hshape),
            pool_s[None])
```

```python
import functools
import math

import jax
import jax.numpy as jnp
import numpy as np
from jax import lax
from jax.experimental import pallas as pl
from jax.experimental.pallas import tpu as pltpu

F32 = jnp.float32
BF16 = jnp.bfloat16

N_HEADS = 8
HEAD_DIM = 64
V_DIM = 2 * HEAD_DIM
ATTN_W = N_HEADS * V_DIM
ROPE_THETA = 10000.0
POOL_WINDOWS = (2, 4, 8, 16)
POOL_GC = 128
POOL_W = len(POOL_WINDOWS) * POOL_GC
POOL_BUF = max(POOL_WINDOWS) - 1
POOL_HALO = 16
MEM_HEADS = 4
MEM_HEAD_DIM = 128
MEM_W = MEM_HEADS * MEM_HEAD_DIM
EPS = 1e-5
NEG_INF = -1e30
Q_SCALE = HEAD_DIM ** -0.5 * math.log2(math.e)
MEM_SCALE = MEM_HEAD_DIM ** -0.5

VMEM_LIMIT = 56 * 1024 * 1024


def _cparams(sem):
    return pltpu.CompilerParams(dimension_semantics=sem, vmem_limit_bytes=VMEM_LIMIT)


def _rms(x, g):
    return x * lax.rsqrt(jnp.mean(x * x, axis=-1, keepdims=True) + EPS) * g


def _dot(a, b):
    return jnp.dot(a, b, preferred_element_type=F32)


def _dot_nt(a, b):
    return lax.dot_general(a, b, (((1,), (1,)), ((), ())), preferred_element_type=F32)


def _row_tile(n, pref):
    t = min(pref, n)
    while n % t:
        t //= 2
    return t


def _in_proj_kernel(x_ref, g_ref, wq_ref, wk_ref, wv_ref, wum_ref, cos_ref, sin_ref,
                    q_ref, kf_ref, kb_ref, vf_ref, vt_ref, u_ref, qm_ref):
    h = _rms(x_ref[...], g_ref[...]).astype(BF16)
    cos = cos_ref[...]
    sin = sin_ref[...]
    lane = lax.broadcasted_iota(jnp.int32, cos.shape, 1)
    first_half = (lane & (HEAD_DIM - 1)) < HEAD_DIM // 2

    def rotary(zh):
        partner = jnp.where(first_half,
                            pltpu.roll(zh, V_DIM - HEAD_DIM // 2, axis=1),
                            pltpu.roll(zh, HEAD_DIM // 2, axis=1))
        return zh * cos + partner * sin

    zq = _dot(h, wq_ref[...])
    for hd in range(N_HEADS):
        sl = slice(hd * V_DIM, (hd + 1) * V_DIM)
        q_ref[:, sl] = (rotary(zq[:, sl]) * Q_SCALE).astype(BF16)
    zk = _dot(h, wk_ref[...])
    for hd in range(N_HEADS):
        sl = slice(hd * V_DIM, (hd + 1) * V_DIM)
        kr = rotary(zk[:, sl])
        kf_ref[:, sl] = kr
        kb_ref[:, sl] = kr.astype(BF16)
    zv = _dot(h, wv_ref[...])
    vf_ref[...] = zv
    for hd in range(N_HEADS):
        sl = slice(hd * V_DIM, (hd + 1) * V_DIM)
        vt_ref[sl, :] = zv[:, sl].T.astype(BF16)
    zu = _dot(h, wum_ref[...])
    u_ref[...] = zu[:, :POOL_W]
    qm_ref[...] = zu[:, POOL_W:].astype(BF16)


def _in_proj(x, g1, w, cos_t, sin_t):
    T, D = x.shape
    P = cos_t.shape[0]
    tm = _row_tile(math.gcd(T, P), 512)
    npos = P // tm
    assert POOL_W + MEM_W == ATTN_W
    row = lambda i: (i, 0)
    const = lambda i: (0, 0)
    posmap = lambda i: (i % npos, 0)
    outs = (
        jax.ShapeDtypeStruct((T, ATTN_W), BF16),
        jax.ShapeDtypeStruct((T, ATTN_W), F32),
        jax.ShapeDtypeStruct((T, ATTN_W), BF16),
        jax.ShapeDtypeStruct((T, ATTN_W), F32),
        jax.ShapeDtypeStruct((ATTN_W, T), BF16),
        jax.ShapeDtypeStruct((T, POOL_W), F32),
        jax.ShapeDtypeStruct((T, MEM_W), BF16),
    )
    return pl.pallas_call(
        _in_proj_kernel,
        out_shape=outs,
        grid=(T // tm,),
        in_specs=[
            pl.BlockSpec((tm, D), row),
            pl.BlockSpec((1, D), const),
            pl.BlockSpec((D, ATTN_W), lambda i: (0, 0)),
            pl.BlockSpec((D, ATTN_W), lambda i: (0, 1)),
            pl.BlockSpec((D, ATTN_W), lambda i: (0, 2)),
            pl.BlockSpec((D, ATTN_W), lambda i: (0, 3)),
            pl.BlockSpec((tm, V_DIM), posmap),
            pl.BlockSpec((tm, V_DIM), posmap),
        ],
        out_specs=(
            pl.BlockSpec((tm, ATTN_W), row),
            pl.BlockSpec((tm, ATTN_W), row),
            pl.BlockSpec((tm, ATTN_W), row),
            pl.BlockSpec((tm, ATTN_W), row),
            pl.BlockSpec((ATTN_W, tm), lambda i: (0, i)),
            pl.BlockSpec((tm, POOL_W), row),
            pl.BlockSpec((tm, MEM_W), row),
        ),
        compiler_params=_cparams(("parallel",)),
        name="in_proj",
    )(x, g1, w, w, w, w, cos_t, sin_t)


def _diff_lambda(lq1_ref, lk1_ref, lq2_ref, lk2_ref, lam_init):
    a = jnp.sum(lq1_ref[...] * lk1_ref[...], axis=-1, keepdims=True)
    b = jnp.sum(lq2_ref[...] * lk2_ref[...], axis=-1, keepdims=True)
    return jnp.exp(a) - jnp.exp(b) + lam_init


def _sub_norm(o, g, lam_init):
    return _rms(o, g) * (1.0 - lam_init)


ONES_ROWS = 16
N_GRP = 16
N_SUB = 4
HEAD_GRP = 4
P_SLOTS = 3
DEC_Q = 4
Q_ROWS = 2 * DEC_Q
SCORE_ROWS = N_HEADS * Q_ROWS
(S_QBLK, S_HEAD, S_NVALID, S_FIRST, S_LAST, S_KBLK) = range(6)


def _prompt_schedule(B, nq, n_steps):
    rows = []
    for b in range(B):
        for h in range(N_HEADS):
            for qi in range(nq):
                blocks = [qi] + list(range(qi))
                n_chunks = -(-len(blocks) // P_SLOTS)
                bounds = [len(blocks) * c // n_chunks for c in range(n_chunks + 1)]
                for c in range(n_chunks):
                    chunk = blocks[bounds[c]:bounds[c + 1]]
                    kb = [b * nq + j for j in chunk]
                    kb += [kb[-1]] * (P_SLOTS - len(chunk))
                    rows.append([b * nq + qi, h, len(chunk), int(c == 0), int(c == n_chunks - 1)] + kb)
    assert len(rows) <= n_steps, (len(rows), n_steps)
    idle = rows[-1][:2] + [0, 0, 0] + rows[-1][S_KBLK:]
    rows += [idle] * (n_steps - len(rows))
    return np.asarray(rows, np.int32).T


def _fused_attn_kernel(pt_ref, sched_ref, q_ref, kn_ref, vn_ref, bias_ref, biasn_ref,
                       lq1_ref, lk1_ref, lq2_ref, lk2_ref, sg_ref, pq_ref, *rest, n_grp, tq, lam_init):
    del pt_ref
    pk_refs = rest[:P_SLOTS]
    pvt_refs = rest[P_SLOTS:2 * P_SLOTS]
    rest = rest[2 * P_SLOTS:]
    k_refs = rest[:n_grp]
    v_refs = rest[n_grp:2 * n_grp]
    o_ref, po_ref = rest[2 * n_grp:2 * n_grp + 2]
    qw_ref, qrows_ref, m_ref, l_ref, acc_ref, pm_ref, pacc_ref = rest[2 * n_grp + 2:2 * n_grp + 9]
    st0_ref, qq_ref = rest[2 * n_grp + 9:]
    g = pl.program_id(1)
    step = pl.program_id(0) * pl.num_programs(1) + g


    @pl.when(g == 0)
    def _():
        q8 = q_ref[0]
        r8 = lax.broadcasted_iota(jnp.int32, (Q_ROWS, V_DIM), 0)
        c8 = lax.broadcasted_iota(jnp.int32, (Q_ROWS, V_DIM), 1)
        own_map = (c8 >= HEAD_DIM) == (r8 >= DEC_Q)
        blocks = [jnp.where(own_map, q8[:, hd * V_DIM:(hd + 1) * V_DIM], 0.0) for hd in range(N_HEADS)]
        qrows_ref[...] = jnp.concatenate(blocks, axis=0).astype(BF16)
        zero = jnp.zeros_like(blocks[0])
        qw_ref[...] = jnp.concatenate(
            [jnp.concatenate([blocks[hd] if hd % HEAD_GRP == a else zero for a in range(HEAD_GRP)], axis=1)
             for hd in range(N_HEADS)], axis=0).astype(BF16)
        m_ref[...] = jnp.full(m_ref.shape, NEG_INF, F32)
        l_ref[...] = jnp.zeros(l_ref.shape, F32)
        acc_ref[...] = jnp.zeros(acc_ref.shape, F32)

    def view(ref):
        n = ref.shape[1] // HEAD_GRP
        return jnp.concatenate([ref[0, pl.ds(a, n, stride=HEAD_GRP), :].astype(BF16)
                                for a in range(HEAD_GRP)], axis=1)

    def update(m, l, acc, s, pv_fn):
        m_new = jnp.maximum(m, jnp.max(s, axis=-1, keepdims=True))
        alpha = jnp.exp2(m - m_new)
        p = jnp.exp2(s - m_new)
        l = alpha * l + jnp.sum(p, axis=-1, keepdims=True)
        acc = alpha * acc + pv_fn(p.astype(BF16))
        return m_new, l, acc

    qw = qw_ref[...]
    bias = bias_ref[...]
    pq = pq_ref[...]
    lane = lax.broadcasted_iota(jnp.int32, pq.shape, 1)
    pzero = jnp.zeros_like(pq)
    qq = jnp.concatenate([jnp.where(lane < HEAD_DIM, pq, pzero),
                          jnp.where(lane >= HEAD_DIM, pq, pzero)], axis=0)
    n = bias.shape[1]
    n_sub = min(N_SUB, n_grp)
    sub = n_grp // n_sub
    qq_ref[...] = qq
    scores = []
    for u in range(n_sub):
        scores.append(jnp.concatenate(
            [_dot_nt(qw, view(k_refs[u * sub + i])) + bias for i in range(sub)], axis=1))
        if u == 0:
            st0_ref[...] = _dot_nt(pk_refs[0][...], qq)
    m, l, acc = m_ref[...], l_ref[...], acc_ref[...]
    for u in range(n_sub):
        def past_pv(p, u=u):
            pv = _dot(p[:, 0:n], view(v_refs[u * sub]))
            for i in range(1, sub):
                pv = pv + _dot(p[:, i * n:(i + 1) * n], view(v_refs[u * sub + i]))
            return pv
        m, l, acc = update(m, l, acc, scores[u], past_pv)
    m_ref[...], l_ref[...], acc_ref[...] = m, l, acc

    @pl.when(g == pl.num_programs(1) - 1)
    def _():
        sn = _dot_nt(qrows_ref[...], kn_ref[0]) + biasn_ref[...]
        _, l2, acc2 = update(m, l, acc, sn,
                             lambda p: jnp.concatenate([_dot(p, vn_ref[0])] * HEAD_GRP, axis=1))
        lam = _diff_lambda(lq1_ref, lk1_ref, lq2_ref, lk2_ref, lam_init)
        on = acc2 / l2
        for hd in range(N_HEADS):
            a = hd % HEAD_GRP
            blk = on[hd * Q_ROWS:(hd + 1) * Q_ROWS, a * V_DIM:(a + 1) * V_DIM]
            o8 = blk - lam * pltpu.roll(blk, DEC_Q, axis=0)
            o_ref[0, :, hd * V_DIM:(hd + 1) * V_DIM] = _sub_norm(o8, sg_ref[...], lam_init)

    nvalid = sched_ref[S_NVALID, step]
    first = sched_ref[S_FIRST, step]
    last = sched_ref[S_LAST, step]

    def consume(st, slot, masked):
        if masked:
            r = lax.broadcasted_iota(jnp.int32, st.shape, 0)
            c = lax.broadcasted_iota(jnp.int32, st.shape, 1)
            c = jnp.where(c >= tq, c - tq, c)
            st = jnp.where(r <= c, st, NEG_INF)
        pm = pm_ref[...]
        pm_new = jnp.maximum(pm, jnp.max(st, axis=0, keepdims=True))
        alpha = jnp.exp2(pm - pm_new)
        pt = jnp.exp2(st - pm_new).astype(BF16)
        vt1 = jnp.concatenate([pvt_refs[slot][...], jnp.ones((ONES_ROWS, tq), BF16)], axis=0)
        pacc_ref[...] = alpha * pacc_ref[...] + _dot(vt1, pt)
        pm_ref[...] = pm_new

    def slots_block(nv, is_first):
        if is_first:
            pm_ref[...] = jnp.full(pm_ref.shape, NEG_INF, F32)
            pacc_ref[...] = jnp.zeros(pacc_ref.shape, F32)
        sts = [st0_ref[...]] + [_dot_nt(pk_refs[slot][...], qq_ref[...]) for slot in range(1, nv)]
        for slot in range(nv):
            consume(sts[slot], slot, is_first and slot == 0)

    for nv in range(1, P_SLOTS + 1):
        for is_first in (True, False):
            pl.when((nvalid == nv) & (first == int(is_first)))(functools.partial(slots_block, nv, is_first))

    @pl.when(last == 1)
    def _():
        pacc = pacc_ref[...]
        on = pacc[:V_DIM] / pacc[V_DIM:V_DIM + 1]
        lam = _diff_lambda(lq1_ref, lk1_ref, lq2_ref, lk2_ref, lam_init)
        o = (on[:, :tq] - lam * on[:, tq:]).T
        po_ref[...] = _sub_norm(o, sg_ref[...], lam_init).astype(po_ref.dtype)


def _fused_attn(page_table, q8, kn, vn, bias, biasn, lq1, lk1, lq2, lk2, sg, ck, cv,
                pq, pk, pvt, B, S, lam_init):
    Bs, n_pages = page_table.shape
    page_rows = ck.shape[1]
    new_rows = kn.shape[1]
    n_grp = N_GRP
    while n_pages % n_grp:
        n_grp //= 2
    n_groups = n_pages // n_grp
    T = pq.shape[0]
    tq = _row_tile(S, 512)
    sched = jnp.asarray(_prompt_schedule(B, S // tq, Bs * n_groups))
    step = lambda b, g: b * n_groups + g
    vec = lambda b, g, pt, sc: (0, 0)
    seq = lambda b, g, pt, sc: (b, 0, 0)
    qmap = lambda b, g, pt, sc: (sc[S_QBLK, step(b, g)], sc[S_HEAD, step(b, g)])

    def page_spec(i):
        return pl.BlockSpec((1, page_rows, V_DIM), lambda b, g, pt, sc: (pt[step(b, g) * n_grp + i], 0, 0))

    def pk_spec(i):
        return pl.BlockSpec((tq, V_DIM), lambda b, g, pt, sc: (sc[S_KBLK + i, step(b, g)], sc[S_HEAD, step(b, g)]))

    def pvt_spec(i):
        return pl.BlockSpec((V_DIM, tq), lambda b, g, pt, sc: (sc[S_HEAD, step(b, g)], sc[S_KBLK + i, step(b, g)]))

    kern = functools.partial(_fused_attn_kernel, n_grp=n_grp, tq=tq, lam_init=lam_init)
    grid_spec = pltpu.PrefetchScalarGridSpec(
        num_scalar_prefetch=2,
        grid=(Bs, n_groups),
        in_specs=[
            pl.BlockSpec((1, Q_ROWS, ATTN_W), seq),
            pl.BlockSpec((1, new_rows, V_DIM), seq),
            pl.BlockSpec((1, new_rows, V_DIM), seq),
            pl.BlockSpec(bias.shape, vec),
            pl.BlockSpec(biasn.shape, vec),
            pl.BlockSpec((1, HEAD_DIM), vec),
            pl.BlockSpec((1, HEAD_DIM), vec),
            pl.BlockSpec((1, HEAD_DIM), vec),
            pl.BlockSpec((1, HEAD_DIM), vec),
            pl.BlockSpec((1, V_DIM), vec),
            pl.BlockSpec((tq, V_DIM), qmap),
        ] + [pk_spec(i) for i in range(P_SLOTS)] + [pvt_spec(i) for i in range(P_SLOTS)]
          + [page_spec(i) for i in range(n_grp)] + [page_spec(i) for i in range(n_grp)],
        out_specs=(pl.BlockSpec((1, Q_ROWS, ATTN_W), seq), pl.BlockSpec((tq, V_DIM), qmap)),
        scratch_shapes=[
            pltpu.VMEM((SCORE_ROWS, HEAD_GRP * V_DIM), BF16),
            pltpu.VMEM((SCORE_ROWS, V_DIM), BF16),
            pltpu.VMEM((SCORE_ROWS, 1), F32),
            pltpu.VMEM((SCORE_ROWS, 1), F32),
            pltpu.VMEM((SCORE_ROWS, HEAD_GRP * V_DIM), F32),
            pltpu.VMEM((1, 2 * tq), F32),
            pltpu.VMEM((V_DIM + ONES_ROWS, 2 * tq), F32),
            pltpu.VMEM((tq, 2 * tq), F32),
            pltpu.VMEM((2 * tq, V_DIM), BF16),
        ],
    )
    return pl.pallas_call(
        kern,
        out_shape=(jax.ShapeDtypeStruct((Bs, Q_ROWS, ATTN_W), F32), jax.ShapeDtypeStruct((T, ATTN_W), BF16)),
        grid_spec=grid_spec,
        compiler_params=_cparams(("arbitrary", "arbitrary")),
        name="fused_attn",
    )(page_table.reshape(-1), sched, q8, kn, vn, bias, biasn, lq1, lk1, lq2, lk2, sg, pq,
      *([pk] * P_SLOTS), *([pvt] * P_SLOTS), *([ck] * n_grp), *([cv] * n_grp))


def _page_bias(rows, cols):
    rj = jnp.arange(rows, dtype=jnp.int32)[:, None]
    cj = jnp.arange(cols, dtype=jnp.int32)[None, :]
    n_hg = N_HEADS // HEAD_GRP
    return jnp.where((cj % n_hg) == (rj // Q_ROWS) // HEAD_GRP, 0.0, NEG_INF).astype(F32)


def _new_bias(rows, cols):
    rj = jnp.arange(rows, dtype=jnp.int32)[:, None]
    cj = jnp.arange(cols, dtype=jnp.int32)[None, :]
    keep = ((cj % N_HEADS) == (rj // Q_ROWS)) & ((cj // N_HEADS) <= (rj % DEC_Q))
    return jnp.where(keep, 0.0, NEG_INF).astype(F32)


def _pool_rows(ext, cnt_fn, pw_ref, ps_ref):
    outs = []
    for gi, w in enumerate(POOL_WINDOWS):
        tok = ext[:, gi * POOL_GC:(gi + 1) * POOL_GC]
        cur = tok
        sh = 1
        while sh < w:
            cur = cur + pltpu.roll(cur, sh, axis=0)
            sh *= 2
        d = cur / cnt_fn(w) - tok
        outs.append(_dot(d.astype(BF16), pw_ref[gi]))
    return jnp.concatenate(outs, axis=-1) * ps_ref[...]


def _mem_attend(qm, mk, mv):
    outs = []
    for hd in range(MEM_HEADS):
        sl = slice(hd * MEM_HEAD_DIM, (hd + 1) * MEM_HEAD_DIM)
        s = _dot_nt(qm[:, sl], mk[:, sl]) * MEM_SCALE
        s = s - jnp.max(s, axis=-1, keepdims=True)
        e = jnp.exp(s)
        p = e / jnp.sum(e, axis=-1, keepdims=True)
        outs.append(_dot(p.astype(BF16), mv[:, sl]))
    return jnp.concatenate(outs, axis=-1)


def _prompt_branch_kernel(u_ref, halo_ref, qm_ref, mk_ref, mv_ref, pw_ref, ps_ref, op_ref, om_ref, *, tm):
    i = pl.program_id(1)
    halo = jnp.where(i > 0, halo_ref[...], 0.0)
    ext = jnp.concatenate([halo, u_ref[...]], axis=0)
    pos = i * tm - POOL_HALO + lax.broadcasted_iota(jnp.int32, (POOL_HALO + tm, 1), 0)
    cnt_fn = lambda w: jnp.clip(pos + 1, 1, w).astype(F32)
    y = _pool_rows(ext, cnt_fn, pw_ref, ps_ref)
    op_ref[...] = y[POOL_HALO:].astype(op_ref.dtype)
    om_ref[...] = _mem_attend(qm_ref[...], mk_ref[...].astype(BF16),
                              mv_ref[...].astype(BF16)).astype(om_ref.dtype)


def _prompt_branch(u, qm, mk, mv, pw, ps, B, S):
    T = u.shape[0]
    n_mem = mk.shape[0] // B
    tm = _row_tile(S, 512)
    nt = S // tm
    hb = tm // POOL_HALO
    row = lambda b, i: (b * nt + i, 0)
    kern = functools.partial(_prompt_branch_kernel, tm=tm)
    return pl.pallas_call(
        kern,
        out_shape=(jax.ShapeDtypeStruct((T, POOL_W), BF16), jax.ShapeDtypeStruct((T, MEM_W), BF16)),
        grid=(B, nt),
        in_specs=[
            pl.BlockSpec((tm, POOL_W), row),
            pl.BlockSpec((POOL_HALO, POOL_W), lambda b, i: (jnp.maximum((b * nt + i) * hb - 1, 0), 0)),
            pl.BlockSpec((tm, MEM_W), row),
            pl.BlockSpec((n_mem, MEM_W), lambda b, i: (b, 0)),
            pl.BlockSpec((n_mem, MEM_W), lambda b, i: (b, 0)),
            pl.BlockSpec((len(POOL_WINDOWS), POOL_GC, POOL_GC), lambda b, i: (0, 0, 0)),
            pl.BlockSpec((1, POOL_W), lambda b, i: (0, 0)),
        ],
        out_specs=(pl.BlockSpec((tm, POOL_W), row), pl.BlockSpec((tm, MEM_W), row)),
        compiler_params=_cparams(("parallel", "arbitrary")),
        name="prompt_branch",
    )(u, u, qm, mk, mv, pw, ps)


def _sample_branch_kernel(ext_ref, qm_ref, mk_ref, mv_ref, pw_ref, ps_ref, op_ref, om_ref, *, nb):
    y = _pool_rows(ext_ref[...], lambda w: float(w), pw_ref, ps_ref)
    op_ref[...] = y.astype(op_ref.dtype)
    shape = (qm_ref.shape[1], mk_ref.shape[1])
    rj = lax.broadcasted_iota(jnp.int32, shape, 0)
    cj = lax.broadcasted_iota(jnp.int32, shape, 1)
    own = (cj & (MEM_HEADS - 1)) == (rj >> 2)
    for n in range(nb):
        s = jnp.where(own, _dot_nt(qm_ref[n], mk_ref[n].astype(BF16)) * MEM_SCALE, NEG_INF)
        e = jnp.exp(s - jnp.max(s, axis=-1, keepdims=True))
        p = e / jnp.sum(e, axis=-1, keepdims=True)
        om_ref[n] = _dot(p.astype(BF16), mv_ref[n].astype(BF16)).astype(om_ref.dtype)


def _sample_branch(ext, qm16, mk, mv, pw, ps, ext_rows):
    Bs, mem_rows, _ = mk.shape
    q_rows = qm16.shape[1]
    nb = _row_tile(Bs, 8)
    kern = functools.partial(_sample_branch_kernel, nb=nb)
    seq3 = lambda i: (i, 0, 0)
    return pl.pallas_call(
        kern,
        out_shape=(jax.ShapeDtypeStruct((Bs * ext_rows, POOL_W), BF16),
                   jax.ShapeDtypeStruct((Bs, q_rows, MEM_HEAD_DIM), BF16)),
        grid=(Bs // nb,),
        in_specs=[
            pl.BlockSpec((nb * ext_rows, POOL_W), lambda i: (i, 0)),
            pl.BlockSpec((nb, q_rows, MEM_HEAD_DIM), seq3),
            pl.BlockSpec((nb, mem_rows, MEM_HEAD_DIM), seq3),
            pl.BlockSpec((nb, mem_rows, MEM_HEAD_DIM), seq3),
            pl.BlockSpec((len(POOL_WINDOWS), POOL_GC, POOL_GC), lambda i: (0, 0, 0)),
            pl.BlockSpec((1, POOL_W), lambda i: (0, 0)),
        ],
        out_specs=(pl.BlockSpec((nb * ext_rows, POOL_W), lambda i: (i, 0)),
                   pl.BlockSpec((nb, q_rows, MEM_HEAD_DIM), seq3)),
        compiler_params=_cparams(("parallel",)),
        name="sample_branch",
    )(ext, qm16, mk, mv, pw, ps)


def _mem_kv_kernel(mem_ref, g_ref, w_ref, k_ref, v_ref):
    h = _rms(mem_ref[...], g_ref[...]).astype(BF16)
    kv = _dot(h, w_ref[...])
    k_ref[...] = kv[:, :MEM_W]
    v_ref[...] = kv[:, MEM_W:]


def _mem_kv(mem, g, w):
    R, D = mem.shape
    tm = _row_tile(R, 256)
    row = lambda i: (i, 0)
    const = lambda i: (0, 0)
    return pl.pallas_call(
        _mem_kv_kernel,
        out_shape=(jax.ShapeDtypeStruct((R, MEM_W), F32), jax.ShapeDtypeStruct((R, MEM_W), F32)),
        grid=(R // tm,),
        in_specs=[pl.BlockSpec((tm, D), row), pl.BlockSpec((1, D), const),
                  pl.BlockSpec((D, 2 * MEM_W), const)],
        out_specs=(pl.BlockSpec((tm, MEM_W), row), pl.BlockSpec((tm, MEM_W), row)),
        compiler_params=_cparams(("parallel",)),
        name="mem_kv",
    )(mem, g, w)


def _merge_kernel(x_ref, g1_ref, wga_ref, wgp_ref, wgm_ref, oa_ref, op_ref, om_ref,
                  wa_ref, wp_ref, wm_ref, wo_ref, x1_ref):
    x = x_ref[...]
    h = _rms(x, g1_ref[...]).astype(BF16)

    def gate(wg_ref):
        return jax.nn.sigmoid(_dot(h, wg_ref[...]))

    mix = gate(wga_ref) * _dot(oa_ref[...], wa_ref[...])
    mix = mix + gate(wgp_ref) * _dot(op_ref[...], wp_ref[...])
    mix = mix + gate(wgm_ref) * _dot(om_ref[...], wm_ref[...])
    x1_ref[...] = x + _dot(mix.astype(BF16), wo_ref[...])


def _merge(x, g1, w_in, oa, op, om, wa, wp, wm, wo):
    T, D = x.shape
    tm = _row_tile(T, 512)
    gate0 = w_in.shape[1] // D - 3
    row = lambda i: (i, 0)
    const = lambda i: (0, 0)
    full = lambda a: pl.BlockSpec(a.shape, const)
    return pl.pallas_call(
        _merge_kernel,
        out_shape=jax.ShapeDtypeStruct((T, D), F32),
        grid=(T // tm,),
        in_specs=[pl.BlockSpec((tm, D), row), full(g1),
                  pl.BlockSpec((D, D), lambda i: (0, gate0)),
                  pl.BlockSpec((D, D), lambda i: (0, gate0 + 1)),
                  pl.BlockSpec((D, D), lambda i: (0, gate0 + 2)),
                  pl.BlockSpec((tm, ATTN_W), row), pl.BlockSpec((tm, POOL_W), row),
                  pl.BlockSpec((tm, MEM_W), row), full(wa), full(wp), full(wm), full(wo)],
        out_specs=pl.BlockSpec((tm, D), row),
        compiler_params=_cparams(("parallel",)),
        name="merge",
    )(x, g1, w_in, w_in, w_in, oa, op, om, wa, wp, wm, wo)


def _ffn_kernel(x1_ref, g2_ref, wg_ref, wu_ref, wd_ref, gf_ref, y_ref):
    x1 = x1_ref[...]
    h2 = _rms(x1, g2_ref[...]).astype(BF16)
    a = _dot(h2, wg_ref[...])
    b = _dot(h2, wu_ref[...])
    t = (jax.nn.silu(a) * b).astype(BF16)
    x2 = x1 + _dot(t, wd_ref[...])
    y_ref[...] = _rms(x2, gf_ref[...])


def _ffn(x1, g2, wg, wu, wd, gf):
    T, D = x1.shape
    tm = _row_tile(T, 256)
    row = lambda i: (i, 0)
    const = lambda i: (0, 0)
    full = lambda a: pl.BlockSpec(a.shape, const)
    return pl.pallas_call(
        _ffn_kernel,
        out_shape=jax.ShapeDtypeStruct((T, D), F32),
        grid=(T // tm,),
        in_specs=[pl.BlockSpec((tm, D), row), full(g2), full(wg), full(wu), full(wd), full(gf)],
        out_specs=pl.BlockSpec((tm, D), row),
        compiler_params=_cparams(("parallel",)),
        name="ffn",
    )(x1, g2, wg, wu, wd, gf)


def _rope_tables(pos):
    half = HEAD_DIM // 2
    inv = ROPE_THETA ** (-jnp.arange(half, dtype=F32) / half)
    ang = pos.astype(F32)[:, None] * inv[None, :]
    cos = jnp.cos(ang)
    sin = jnp.sin(ang)
    return jnp.tile(cos, (1, 4)), jnp.concatenate([-sin, sin, -sin, sin], axis=1)


def kernel(x_prompt, x_sample, cache_k, cache_v, page_table, state_pool, cache_mem_k, cache_mem_v,
           mem_prompt, norm1_g, w_in, lambda_q1, lambda_k1, lambda_q2, lambda_k2, subln_g,
           pool_w, pool_scale, norm_mem_g, w_mem_kv, w_attn_proj, w_pool_proj, w_mem_proj, w_out,
           norm2_g, w_ffn_gate, w_ffn_up, w_ffn_down, norm_f_g):
    B, S, D = x_prompt.shape
    Bs, Ss, _ = x_sample.shape
    assert Ss == DEC_Q and D == ATTN_W
    depth = w_in.shape[0]
    assert depth == 1
    n_pool, page = cache_k.shape[1], cache_k.shape[2]
    past_len = page_table.shape[1] * page
    n_mem = mem_prompt.shape[1]
    row2 = lambda a: a.reshape(1, -1)

    cos_p, sin_p = _rope_tables(jnp.arange(S, dtype=jnp.int32))
    cos_s, sin_s = _rope_tables(past_len + jnp.arange(Ss, dtype=jnp.int32))
    cos_s = jnp.tile(cos_s, (Bs, 1))
    sin_s = jnp.tile(sin_s, (Bs, 1))

    l = 0
    lam_init = 0.8 - 0.6 * math.exp(-0.3 * l)
    w_in_b = w_in[l].astype(BF16)
    g1 = row2(norm1_g[l])
    lam_args = (row2(lambda_q1[l]), row2(lambda_k1[l]), row2(lambda_q2[l]), row2(lambda_k2[l]),
                row2(subln_g[l]))
    pw = pool_w[l].astype(BF16)
    ps = row2(pool_scale[l])
    dense_w = (w_attn_proj[l].astype(BF16), w_pool_proj[l].astype(BF16), w_mem_proj[l].astype(BF16),
               w_out[l].astype(BF16))
    ffn_w = (row2(norm2_g[l]), w_ffn_gate[l].astype(BF16), w_ffn_up[l].astype(BF16),
             w_ffn_down[l].astype(BF16), row2(norm_f_g))

    xp = x_prompt.reshape(B * S, D)
    xs = x_sample.reshape(Bs * Ss, D)
    mk_p, mv_p = _mem_kv(mem_prompt.reshape(B * n_mem, D), row2(norm_mem_g[l]), w_mem_kv[l].astype(BF16))
    q_p, kf_p, kb_p, vf_p, vt_p, u_p, qm_p = _in_proj(xp, g1, w_in_b, cos_p, sin_p)
    q_s, kf_s, kb_s, vf_s, _, u_s, qm_s = _in_proj(xs, g1, w_in_b, cos_s, sin_s)

    q_s3 = q_s.reshape(Bs, Ss, ATTN_W).astype(F32)
    q8 = jnp.concatenate([q_s3, q_s3], axis=1)
    head_rows = lambda a, n: a.reshape(n, -1, V_DIM)
    oa_s8, oa_p = _fused_attn(page_table, q8, head_rows(kb_s, Bs), head_rows(vf_s.astype(BF16), Bs),
                              _page_bias(SCORE_ROWS, page * N_HEADS // HEAD_GRP),
                              _new_bias(SCORE_ROWS, Ss * N_HEADS), *lam_args,
                              head_rows(cache_k[l], n_pool), head_rows(cache_v[l], n_pool),
                              q_p, kb_p, vt_p, B, S, lam_init)
    oa_s = oa_s8[:, :Ss].reshape(Bs * Ss, ATTN_W).astype(BF16)

    op_p, om_p = _prompt_branch(u_p, qm_p, mk_p, mv_p, pw, ps, B, S)
    x1_p = _merge(xp, g1, w_in_b, oa_p, op_p, om_p, *dense_w)
    y_p = _ffn(x1_p, *ffn_w)

    ext_rows = 1 + POOL_BUF + Ss + 4
    u_s3 = u_s.reshape(Bs, Ss, POOL_W)
    ext = jnp.concatenate([jnp.zeros((Bs, 1, POOL_W), F32), state_pool[l], u_s3,
                           jnp.zeros((Bs, 4, POOL_W), F32)], axis=1).reshape(Bs * ext_rows, POOL_W)
    qm_hq = qm_s.reshape(Bs, Ss, MEM_HEADS, MEM_HEAD_DIM).transpose(0, 2, 1, 3)
    op_s_full, om_hq = _sample_branch(ext, qm_hq.reshape(Bs, MEM_HEADS * Ss, MEM_HEAD_DIM),
                                      cache_mem_k[l].reshape(Bs, n_mem * MEM_HEADS, MEM_HEAD_DIM),
                                      cache_mem_v[l].reshape(Bs, n_mem * MEM_HEADS, MEM_HEAD_DIM),
                                      pw, ps, ext_rows)
    op_s = op_s_full.reshape(Bs, ext_rows, POOL_W)[:, 1 + POOL_BUF:1 + POOL_BUF + Ss].reshape(Bs * Ss, POOL_W)
    om_s = om_hq.reshape(Bs, MEM_HEADS, Ss, MEM_HEAD_DIM).transpose(0, 2, 1, 3).reshape(Bs * Ss, MEM_W)
    x1_s = _merge(xs, g1, w_in_b, oa_s, op_s, om_s, *dense_w)
    y_s = _ffn(x1_s, *ffn_w)

    pool_p = u_p.reshape(B, S, POOL_W)[:, S - POOL_BUF:]
    pool_s = jnp.concatenate([state_pool[l][:, Ss:], u_s3], axis=1)
    hshape = (N_HEADS, V_DIM)
    mshape = (MEM_HEADS, MEM_HEAD_DIM)
    return (y_p.reshape(B, S, D), y_s.reshape(Bs, Ss, D),
            kf_p.reshape(1, B, S, *hshape), vf_p.reshape(1, B, S, *hshape),
            pool_p[None],
            mk_p.reshape(1, B, n_mem, *mshape), mv_p.reshape(1, B, n_mem, *mshape),
            kf_s.reshape(1, Bs, Ss, *hshape), vf_s.reshape(1, Bs, Ss, *hshape),
            pool_s[None])
```
